```python
import math
import jax, jax.numpy as jnp
from jax import lax
import numpy as np

D_MODEL = 2048
BATCH = 2
SEQ = 8192
DEPTH = 4

GRID_W = 64
CTX_LEN = 256
MIXER_ORDER = ('mla', 'gdn', 'mlstm')
N_MIXERS = len(MIXER_ORDER)
MLP_HIDDEN = 4 * D_MODEL
N_MOD = 6
EPS = 1e-6
ROPE_THETA = 10000.0

MLA_HEADS = 16
MLA_Q_RANK = 768
MLA_KV_RANK = 512
MLA_NOPE = 128
MLA_ROPE = 64
MLA_V = 128
Q_BLOCK = 128

GDN_K_HEADS = 16
GDN_V_HEADS = 32
GDN_DK = 128
GDN_DV = 128
GDN_CONV = 5
GDN_CHUNK = 64

ML_HEADS = 8
ML_DQK = 128
ML_DV = 256
ML_CHUNK = 64

kernel_name = 'hybrid_mla_gdn_mlstm_flow_block'


def rmsnorm(x, g):
    xf = x.astype(jnp.float32)
    y = xf * lax.rsqrt(jnp.mean(xf * xf, -1, keepdims=True) + EPS)
    return (y * g.astype(jnp.float32)).astype(x.dtype)


def l2norm(x):
    xf = x.astype(jnp.float32)
    return (xf * lax.rsqrt(jnp.sum(xf * xf, -1, keepdims=True) + EPS)).astype(x.dtype)


def modulate(h, shift, scale):
    return h * (1 + scale[:, None, :]) + shift[:, None, :]


def sq_relu_mlp(h, w1, w2):
    return jnp.square(jax.nn.relu(h @ w1)) @ w2


def axial_rope_tables(n):
    rows = n // GRID_W
    t = jnp.arange(rows * GRID_W)
    row = (t // GRID_W).astype(jnp.float32)
    col = (t % GRID_W).astype(jnp.float32)
    axis_dim = MLA_ROPE // 2
    freqs = jnp.power(ROPE_THETA, -jnp.arange(0, axis_dim, 2, dtype=jnp.float32) / axis_dim)
    ar = row[:, None] * freqs[None, :]
    ac = col[:, None] * freqs[None, :]
    return (jnp.cos(ar), jnp.sin(ar), jnp.cos(ac), jnp.sin(ac))


def _rotate(x, cos, sin):
    x1, x2 = jnp.split(x, 2, -1)
    return jnp.concatenate([x1 * cos - x2 * sin, x1 * sin + x2 * cos], -1)


def axial_rope(x, tabs):
    cr, sr, cc, sc = (t[:, None, :].astype(x.dtype) for t in tabs)
    xr, xc = jnp.split(x, 2, -1)
    return jnp.concatenate([_rotate(xr, cr, sr), _rotate(xc, cc, sc)], -1)


def mla_project(h, w_in, q_norm, kv_norm, w_q_up, w_kv_up):
    B, L, _ = h.shape
    cq, ckv, k_rope = jnp.split(h @ w_in, [MLA_Q_RANK, MLA_Q_RANK + MLA_KV_RANK], -1)
    q = (rmsnorm(cq, q_norm) @ w_q_up).reshape(B, L, MLA_HEADS, MLA_NOPE + MLA_ROPE)
    kv = (rmsnorm(ckv, kv_norm) @ w_kv_up).reshape(B, L, MLA_HEADS, MLA_NOPE + MLA_V)
    k_nope, v = jnp.split(kv, [MLA_NOPE], -1)
    return q, k_nope, k_rope[:, :, None, :], v


def mla_keys(k_nope, k_rope):
    return jnp.concatenate([k_nope, jnp.broadcast_to(k_rope, k_nope.shape[:3] + (MLA_ROPE,))], -1)


def block_attention(q, k, v):
    B, n, H, Dq = q.shape
    nb = n // Q_BLOCK
    scale = Dq ** -0.5
    qb = q.reshape(B, nb, Q_BLOCK, H, Dq).transpose(1, 0, 2, 3, 4)

    def one_block(qi):
        s = jnp.einsum('bqhd,bkhd->bhqk', qi, k).astype(jnp.float32) * scale
        p = jax.nn.softmax(s, -1).astype(v.dtype)
        return jnp.einsum('bhqk,bkhd->bqhd', p, v)

    o = lax.map(one_block, qb)
    return o.transpose(1, 0, 2, 3, 4).reshape(B, n, H, v.shape[-1])


def mla_mixer(ux, uc, w_in, q_norm, kv_norm, w_q_up, w_kv_up, w_o, tabs, ctx_out):
    B, L, _ = ux.shape
    qx, knx, krx, vx = mla_project(ux, w_in, q_norm, kv_norm, w_q_up, w_kv_up)
    qc, knc, krc, vc = mla_project(uc, w_in, q_norm, kv_norm, w_q_up, w_kv_up)
    qx = jnp.concatenate([qx[..., :MLA_NOPE], axial_rope(qx[..., MLA_NOPE:], tabs)], -1)
    kx = mla_keys(knx, axial_rope(krx, tabs))
    kc = mla_keys(knc, krc)
    ox = block_attention(qx, jnp.concatenate([kx, kc], 1), jnp.concatenate([vx, vc], 1))
    yx = ox.reshape(B, L, MLA_HEADS * MLA_V) @ w_o
    if not ctx_out:
        return yx, None
    oc = block_attention(qc, kc, vc)
    return yx, oc.reshape(B, uc.shape[1], MLA_HEADS * MLA_V) @ w_o


def to_chunks(t, size):
    B, L = t.shape[:2]
    t = t.reshape((B, L // size, size) + t.shape[2:])
    return t.transpose((1, 0, 3, 2) + tuple(range(4, t.ndim)))


def from_chunks(t):
    nc, B, H, size = t.shape[:4]
    t = t.transpose((1, 0, 3, 2) + tuple(range(4, t.ndim)))
    return t.reshape((B, nc * size, H) + t.shape[4:])


def _orient(t, d):
    return jnp.flip(t, 1) if d == 1 else t


def centred_conv(u, w):
    pad = w.shape[0] // 2
    return lax.conv_general_dilated(u, w[:, None, :].astype(u.dtype), (1,), [(pad, pad)],
                                    dimension_numbers=('NWC', 'WIO', 'NWC'),
                                    feature_group_count=u.shape[-1])


def gdn_project(h, w_in, conv_w, a_log, dt_bias):
    B, L, _ = h.shape
    qk_w = GDN_K_HEADS * GDN_DK
    v_w = GDN_V_HEADS * GDN_DV
    qkv, z, gates = jnp.split(h @ w_in, [2 * qk_w + v_w, 2 * qk_w + 2 * v_w], -1)
    qkv = jax.nn.silu(centred_conv(qkv, conv_w))
    q, k, v = jnp.split(qkv, [qk_w, 2 * qk_w], -1)
    rep = GDN_V_HEADS // GDN_K_HEADS
    q = jnp.repeat(l2norm(q.reshape(B, L, GDN_K_HEADS, GDN_DK)), rep, axis=2) * GDN_DK ** -0.5
    k = jnp.repeat(l2norm(k.reshape(B, L, GDN_K_HEADS, GDN_DK)), rep, axis=2)
    v = v.reshape(B, L, GDN_V_HEADS, GDN_DV)
    gates = gates.astype(jnp.float32).reshape(B, L, 2, 2, GDN_V_HEADS)
    beta = jax.nn.sigmoid(gates[:, :, :, 0])
    g = -jnp.exp(a_log.astype(jnp.float32)) * jax.nn.softplus(gates[:, :, :, 1] + dt_bias.astype(jnp.float32))
    return q, k, v, z.reshape(B, L, GDN_V_HEADS, GDN_DV), beta, g


def gated_delta_chunked(q, k, v, beta, g, state):
    out_dtype = v.dtype
    C = GDN_CHUNK
    qc, kc, vc = (to_chunks(t.astype(jnp.float32), C) for t in (q, k, v))
    bc = to_chunks(beta.astype(jnp.float32), C)
    gcum = jnp.cumsum(to_chunks(g.astype(jnp.float32), C), -1)
    tri = jnp.tril(jnp.ones((C, C), bool))
    stri = jnp.tril(jnp.ones((C, C), bool), -1)
    decay = jnp.exp(jnp.where(tri, gcum[..., :, None] - gcum[..., None, :], -jnp.inf))
    kb = kc * bc[..., None]
    n_mat = -jnp.where(stri, jnp.einsum('...id,...jd->...ij', kb, kc) * decay, 0.0)
    t_mat = jnp.eye(C, dtype=jnp.float32) + n_mat
    p_mat = n_mat
    for _ in range(int(math.log2(C)) - 1):
        p_mat = p_mat @ p_mat
        t_mat = t_mat + t_mat @ p_mat
    value = t_mat @ (vc * bc[..., None])
    k_cum = t_mat @ (kb * jnp.exp(gcum)[..., None])
    attn = jnp.einsum('...id,...jd->...ij', qc, kc) * decay
    q_dec = qc * jnp.exp(gcum)[..., None]
    k_tail = kc * jnp.exp(gcum[..., -1:] - gcum)[..., None]
    a_last = jnp.exp(gcum[..., -1])

    def step(s, inp):
        val, kcm, att, qd, kt, al = inp
        v_new = val - kcm @ s
        o = qd @ s + att @ v_new
        s = s * al[..., None, None] + jnp.swapaxes(kt, -1, -2) @ v_new
        return s, o

    state, o = lax.scan(step, state, (value, k_cum, attn, q_dec, k_tail, a_last))
    return from_chunks(o).astype(out_dtype), state


def gdn_output(o, z, out_norm, w_o):
    B, L = o.shape[:2]
    o = rmsnorm(o, out_norm) * jax.nn.silu(z)
    return o.reshape(B, L, GDN_V_HEADS * GDN_DV) @ w_o


def gdn_mixer(ux, uc, w_in, conv_w, a_log, dt_bias, out_norm, w_o, ctx_out):
    qx, kx, vx, zx, bx, gx = gdn_project(ux, w_in, conv_w, a_log, dt_bias)
    qc, kc, vc, zc, bc, gc = gdn_project(uc, w_in, conv_w, a_log, dt_bias)
    B = ux.shape[0]
    outs_x, outs_c = [], []
    for d in range(2):
        s0 = jnp.zeros((B, GDN_V_HEADS, GDN_DK, GDN_DV), jnp.float32)
        oc, s_ctx = gated_delta_chunked(*(_orient(t, d) for t in (qc, kc, vc, bc[:, :, d], gc[:, :, d])), s0)
        ox, _ = gated_delta_chunked(*(_orient(t, d) for t in (qx, kx, vx, bx[:, :, d], gx[:, :, d])), s_ctx)
        outs_x.append(_orient(ox, d))
        outs_c.append(_orient(oc, d))
    yx = gdn_output(outs_x[0] + outs_x[1], zx, out_norm, w_o)
    yc = gdn_output(outs_c[0] + outs_c[1], zc, out_norm, w_o) if ctx_out else None
    return yx, yc


def mlstm_project(h, w_in, gate_b):
    B, L, _ = h.shape
    qk_w = ML_HEADS * ML_DQK
    v_w = ML_HEADS * ML_DV
    q, k, v, o, gates = jnp.split(h @ w_in, [qk_w, 2 * qk_w, 2 * qk_w + v_w, 2 * qk_w + 2 * v_w], -1)
    q = q.reshape(B, L, ML_HEADS, ML_DQK)
    k = k.reshape(B, L, ML_HEADS, ML_DQK) * ML_DQK ** -0.5
    v = v.reshape(B, L, ML_HEADS, ML_DV)
    gates = gates.astype(jnp.float32).reshape(B, L, 2, 2, ML_HEADS) + gate_b.astype(jnp.float32)
    log_i = gates[:, :, :, 0]
    log_f = jax.nn.log_sigmoid(gates[:, :, :, 1])
    return q, k, v, o.reshape(B, L, ML_HEADS, ML_DV), log_i, log_f


def mlstm_chunked(q, k, v, log_i, log_f, state):
    out_dtype = v.dtype
    Cs = ML_CHUNK
    qc, kc, vc = (to_chunks(t.astype(jnp.float32), Cs) for t in (q, k, v))
    ic = to_chunks(log_i.astype(jnp.float32), Cs)
    b = jnp.cumsum(to_chunks(log_f.astype(jnp.float32), Cs), -1)
    tri = jnp.tril(jnp.ones((Cs, Cs), bool))
    d_mat = jnp.where(tri, b[..., :, None] - b[..., None, :] + ic[..., None, :], -jnp.inf)
    d_max = jnp.max(d_mat, -1)
    qk = jnp.einsum('...id,...jd->...ij', qc, kc)
    w_end = b[..., -1:] - b + ic
    w_end_max = jnp.max(w_end, -1)
    b_last = b[..., -1]

    def step(carry, inp):
        c_mat, n_vec, m = carry
        qi, ki, vi, bi, di, dmi, qki, wi, wmi, bli = inp
        inter = bi + m[..., None]
        mt = jnp.maximum(dmi, inter)
        p = jnp.exp(di - mt[..., None]) * qki
        a_in = jnp.exp(inter - mt)
        num = p @ vi + a_in[..., None] * (qi @ c_mat)
        den = jnp.sum(p, -1) + a_in * jnp.einsum('bhcd,bhd->bhc', qi, n_vec)
        h = num / jnp.maximum(jnp.abs(den), jnp.exp(-mt))[..., None]
        m_new = jnp.maximum(bli + m, wmi)
        carry_decay = jnp.exp(bli + m - m_new)
        kw = ki * jnp.exp(wi - m_new[..., None])[..., None]
        c_mat = carry_decay[..., None, None] * c_mat + jnp.swapaxes(kw, -1, -2) @ vi
        n_vec = carry_decay[..., None] * n_vec + jnp.sum(kw, -2)
        return (c_mat, n_vec, m_new), h

    state, h = lax.scan(step, state, (qc, kc, vc, b, d_mat, d_max, qk, w_end, w_end_max, b_last))
    return from_chunks(h).astype(out_dtype), state


def mlstm_output(h, o, out_norm, w_o):
    B, L = h.shape[:2]
    h = rmsnorm(h, out_norm) * jax.nn.sigmoid(o)
    return h.reshape(B, L, ML_HEADS * ML_DV) @ w_o


def mlstm_mixer(ux, uc, w_in, gate_b, out_norm, w_o, ctx_out):
    qx, kx, vx, ox_gate, ix, fx = mlstm_project(ux, w_in, gate_b)
    qc, kc, vc, oc_gate, ic, fc = mlstm_project(uc, w_in, gate_b)
    B = ux.shape[0]
    outs_x, outs_c = [], []
    for d in range(2):
        s0 = (jnp.zeros((B, ML_HEADS, ML_DQK, ML_DV), jnp.float32),
              jnp.zeros((B, ML_HEADS, ML_DQK), jnp.float32),
              jnp.zeros((B, ML_HEADS), jnp.float32))
        hc_d, s_ctx = mlstm_chunked(*(_orient(t, d) for t in (qc, kc, vc, ic[:, :, d], fc[:, :, d])), s0)
        hx_d, _ = mlstm_chunked(*(_orient(t, d) for t in (qx, kx, vx, ix[:, :, d], fx[:, :, d])), s_ctx)
        outs_x.append(_orient(hx_d, d))
        outs_c.append(_orient(hc_d, d))
    yx = mlstm_output(outs_x[0] + outs_x[1], ox_gate, out_norm, w_o)
    yc = mlstm_output(outs_c[0] + outs_c[1], oc_gate, out_norm, w_o) if ctx_out else None
    return yx, yc


def setup_inputs(seed: int = 0) -> dict:
    key = jax.random.key(seed)
    keys = iter(jax.random.split(key, 160))
    D = D_MODEL

    def nrm(shape, scale=1.0):
        return jax.random.normal(next(keys), shape, jnp.float32) * scale

    def gain(shape):
        return 1.0 + nrm(shape, 0.02)

    p = {'x': nrm((BATCH, SEQ, D)), 'c': nrm((BATCH, D)),
         'ctx': nrm((BATCH, CTX_LEN, D)), 'c_ctx': nrm((D,))}
    for i in range(DEPTH):
        kind = MIXER_ORDER[i % N_MIXERS]
        pre = 'l%d_' % i
        p[pre + 'ada_w'] = nrm((D, N_MOD * D), 0.5 * D ** -0.5)
        p[pre + 'ada_b'] = nrm((N_MOD * D,), 0.02)
        p[pre + 'norm1'] = gain((D,))
        if kind == 'mla':
            p[pre + 'mla_w_in'] = nrm((D, MLA_Q_RANK + MLA_KV_RANK + MLA_ROPE), D ** -0.5)
            p[pre + 'mla_q_norm'] = gain((MLA_Q_RANK,))
            p[pre + 'mla_kv_norm'] = gain((MLA_KV_RANK,))
            p[pre + 'mla_w_q_up'] = nrm((MLA_Q_RANK, MLA_HEADS * (MLA_NOPE + MLA_ROPE)), MLA_Q_RANK ** -0.5)
            p[pre + 'mla_w_kv_up'] = nrm((MLA_KV_RANK, MLA_HEADS * (MLA_NOPE + MLA_V)), MLA_KV_RANK ** -0.5)
            p[pre + 'mla_w_o'] = nrm((MLA_HEADS * MLA_V, D), (MLA_HEADS * MLA_V) ** -0.5)
        elif kind == 'gdn':
            width_in = 2 * GDN_K_HEADS * GDN_DK + 2 * GDN_V_HEADS * GDN_DV + 4 * GDN_V_HEADS
            p[pre + 'gdn_w_in'] = nrm((D, width_in), D ** -0.5)
            p[pre + 'gdn_conv_w'] = nrm((GDN_CONV, 2 * GDN_K_HEADS * GDN_DK + GDN_V_HEADS * GDN_DV), GDN_CONV ** -0.5)
            p[pre + 'gdn_a_log'] = jnp.log(jax.random.uniform(next(keys), (2, GDN_V_HEADS), jnp.float32, 1.0, 16.0))
            dt = jnp.exp(jax.random.uniform(next(keys), (2, GDN_V_HEADS), jnp.float32, math.log(1e-3), math.log(1e-1)))
            p[pre + 'gdn_dt_bias'] = dt + jnp.log(-jnp.expm1(-dt))
            p[pre + 'gdn_out_norm'] = gain((GDN_DV,))
            p[pre + 'gdn_w_o'] = nrm((GDN_V_HEADS * GDN_DV, D), (GDN_V_HEADS * GDN_DV) ** -0.5)
        else:
            width_in = 2 * ML_HEADS * ML_DQK + 2 * ML_HEADS * ML_DV + 4 * ML_HEADS
            p[pre + 'mlstm_w_in'] = nrm((D, width_in), D ** -0.5)
            p[pre + 'mlstm_gate_b'] = jnp.stack([nrm((2, ML_HEADS), 0.1), 3.0 + nrm((2, ML_HEADS), 0.5)], axis=1)
            p[pre + 'mlstm_out_norm'] = gain((ML_HEADS, ML_DV))
            p[pre + 'mlstm_w_o'] = nrm((ML_HEADS * ML_DV, D), (ML_HEADS * ML_DV) ** -0.5)
        p[pre + 'norm2'] = gain((D,))
        p[pre + 'mlp_w1'] = nrm((D, MLP_HIDDEN), D ** -0.5)
        p[pre + 'mlp_w2'] = nrm((MLP_HIDDEN, D), MLP_HIDDEN ** -0.5)
    p['final_norm'] = gain((D,))
    return p


def reference(x, c, ctx, c_ctx,
              l0_ada_w, l0_ada_b, l0_norm1, l0_mla_w_in, l0_mla_q_norm, l0_mla_kv_norm, l0_mla_w_q_up,
              l0_mla_w_kv_up, l0_mla_w_o, l0_norm2, l0_mlp_w1, l0_mlp_w2,
              l1_ada_w, l1_ada_b, l1_norm1, l1_gdn_w_in, l1_gdn_conv_w, l1_gdn_a_log, l1_gdn_dt_bias,
              l1_gdn_out_norm, l1_gdn_w_o, l1_norm2, l1_mlp_w1, l1_mlp_w2,
              l2_ada_w, l2_ada_b, l2_norm1, l2_mlstm_w_in, l2_mlstm_gate_b, l2_mlstm_out_norm, l2_mlstm_w_o,
              l2_norm2, l2_mlp_w1, l2_mlp_w2,
              l3_ada_w, l3_ada_b, l3_norm1, l3_mla_w_in, l3_mla_q_norm, l3_mla_kv_norm, l3_mla_w_q_up,
              l3_mla_w_kv_up, l3_mla_w_o, l3_norm2, l3_mlp_w1, l3_mlp_w2,
              final_norm):
    layers = (
        ((l0_ada_w, l0_ada_b, l0_norm1, l0_norm2, l0_mlp_w1, l0_mlp_w2),
         (l0_mla_w_in, l0_mla_q_norm, l0_mla_kv_norm, l0_mla_w_q_up, l0_mla_w_kv_up, l0_mla_w_o)),
        ((l1_ada_w, l1_ada_b, l1_norm1, l1_norm2, l1_mlp_w1, l1_mlp_w2),
         (l1_gdn_w_in, l1_gdn_conv_w, l1_gdn_a_log, l1_gdn_dt_bias, l1_gdn_out_norm, l1_gdn_w_o)),
        ((l2_ada_w, l2_ada_b, l2_norm1, l2_norm2, l2_mlp_w1, l2_mlp_w2),
         (l2_mlstm_w_in, l2_mlstm_gate_b, l2_mlstm_out_norm, l2_mlstm_w_o)),
        ((l3_ada_w, l3_ada_b, l3_norm1, l3_norm2, l3_mlp_w1, l3_mlp_w2),
         (l3_mla_w_in, l3_mla_q_norm, l3_mla_kv_norm, l3_mla_w_q_up, l3_mla_w_kv_up, l3_mla_w_o)),
    )
    tabs = axial_rope_tables(x.shape[1])
    s_lat = jax.nn.silu(c)
    s_ctx = jax.nn.silu(c_ctx)[None, :]
    hx, hc = x, ctx
    for i in range(DEPTH):
        (ada_w, ada_b, norm1, norm2, mlp_w1, mlp_w2), mixer_w = layers[i]
        kind = MIXER_ORDER[i % N_MIXERS]
        ctx_out = i < DEPTH - 1
        mx = jnp.split(s_lat @ ada_w + ada_b, N_MOD, -1)
        mc = jnp.split(s_ctx @ ada_w + ada_b, N_MOD, -1)
        ux = modulate(rmsnorm(hx, norm1), mx[0], mx[1])
        uc = modulate(rmsnorm(hc, norm1), mc[0], mc[1])
        if kind == 'mla':
            yx, yc = mla_mixer(ux, uc, *mixer_w, tabs, ctx_out)
        elif kind == 'gdn':
            yx, yc = gdn_mixer(ux, uc, *mixer_w, ctx_out)
        else:
            yx, yc = mlstm_mixer(ux, uc, *mixer_w, ctx_out)
        hx = hx + mx[2][:, None, :] * yx
        hx = hx + mx[5][:, None, :] * sq_relu_mlp(modulate(rmsnorm(hx, norm2), mx[3], mx[4]), mlp_w1, mlp_w2)
        if ctx_out:
            hc = hc + mc[2][:, None, :] * yc
            hc = hc + mc[5][:, None, :] * sq_relu_mlp(modulate(rmsnorm(hc, norm2), mc[3], mc[4]), mlp_w1, mlp_w2)
    return rmsnorm(hx, final_norm)
```

```python
import functools
import math

import jax
import jax.numpy as jnp
from jax import lax
from jax.experimental import pallas as pl
from jax.experimental.pallas import tpu as pltpu

F32 = jnp.float32
BF16 = jnp.bfloat16

EPS = 1e-6
ROPE_THETA = 10000.0
GRID_W = 64
N_MOD = 6

MLA_HEADS = 16
MLA_Q_RANK = 768
MLA_KV_RANK = 512
MLA_NOPE = 128
MLA_ROPE = 64
MLA_V = 128
MLA_QK_PAD = 256

GDN_K_HEADS = 16
GDN_V_HEADS = 32
GDN_DK = 128
GDN_DV = 128
GDN_CONV = 5
GDN_GROUP = 4

ML_HEADS = 8
ML_DQK = 128
ML_DV = 256
ML_GROUP = 4

CHUNK = 64
LANES = 128
SUBLANES = 8

VMEM_LIMIT = 52 * 1024 * 1024


def _cparams(sem):
    return pltpu.CompilerParams(dimension_semantics=sem, vmem_limit_bytes=VMEM_LIMIT)


def _dot(a, b):
    return jnp.dot(a, b, preferred_element_type=F32)


def _dot_nt(a, b):
    return lax.dot_general(a, b, (((1,), (1,)), ((), ())), preferred_element_type=F32)


def _dot_tn(a, b):
    return lax.dot_general(a, b, (((0,), (0,)), ((), ())), preferred_element_type=F32)


def _normmod(x, g, shift, scale):
    y = x * lax.rsqrt(jnp.mean(x * x, axis=-1, keepdims=True) + EPS) * g
    return y * (1.0 + scale) + shift


def _sigmoid(x):
    return 1.0 / (1.0 + jnp.exp(-x))


def _softplus(x):
    return jnp.maximum(x, 0.0) + jnp.log(1.0 + jnp.exp(-jnp.abs(x)))


def _log_sigmoid(x):
    return -_softplus(-x)


def _ada_kernel(c_ref, w_ref, b_ref, o_ref):
    c = c_ref[...]
    s = (c * _sigmoid(c)).astype(BF16)
    o_ref[...] = _dot(s, w_ref[...].astype(BF16)) + b_ref[...]


def _ada(cond, w, b):
    rows, d = cond.shape
    n = w.shape[1]
    tn = 1024
    return pl.pallas_call(
        _ada_kernel,
        out_shape=jax.ShapeDtypeStruct((rows, n), F32),
        grid=(n // tn,),
        in_specs=[pl.BlockSpec((rows, d), lambda j: (0, 0)),
                  pl.BlockSpec((d, tn), lambda j: (0, j)),
                  pl.BlockSpec((1, tn), lambda j: (0, j))],
        out_specs=pl.BlockSpec((rows, tn), lambda j: (0, j)),
        compiler_params=_cparams(("arbitrary",)),
        name="ada_mod",
    )(cond, w, b.reshape(1, n))


def _nm_matmul_kernel(x_ref, g_ref, sh_ref, sc_ref, w_ref, o_ref, u_ref):
    @pl.when(pl.program_id(1) == 0)
    def _():
        u_ref[...] = _normmod(x_ref[...], g_ref[...], sh_ref[0], sc_ref[0]).astype(BF16)

    o_ref[...] = _dot(u_ref[...], w_ref[...]).astype(o_ref.dtype)


def _mod_spec(dims, tm, kind, ngrid):
    d, seq, batch = dims
    if ngrid == 1:
        return pl.BlockSpec((1, 1, d), lambda i: (jnp.minimum(i * tm // seq, batch), 0, kind))
    return pl.BlockSpec((1, 1, d), lambda i, j: (jnp.minimum(i * tm // seq, batch), 0, kind))


def _nm_matmul(x, g, mods, kinds, w, dims, rows, tm, tn, out_dtype=F32, name="nm_matmul"):
    d = dims[0]
    n = w.shape[1]
    return pl.pallas_call(
        _nm_matmul_kernel,
        out_shape=jax.ShapeDtypeStruct((rows, n), out_dtype),
        grid=(rows // tm, n // tn),
        in_specs=[pl.BlockSpec((tm, d), lambda i, j: (i, 0)),
                  pl.BlockSpec((1, d), lambda i, j: (0, 0)),
                  _mod_spec(dims, tm, kinds[0], 2),
                  _mod_spec(dims, tm, kinds[1], 2),
                  pl.BlockSpec((d, tn), lambda i, j: (0, j))],
        out_specs=pl.BlockSpec((tm, tn), lambda i, j: (i, j)),
        scratch_shapes=[pltpu.VMEM((tm, d), BF16)],
        compiler_params=_cparams(("parallel", "arbitrary")),
        name=name,
    )(x, g.reshape(1, d), mods, mods, w)


def _mlp_kernel(x_ref, g_ref, sh_ref, sc_ref, gate_ref, w1_ref, w2_ref, o_ref, u_ref, acc_ref):
    k = pl.program_id(1)

    @pl.when(k == 0)
    def _():
        u_ref[...] = _normmod(x_ref[...], g_ref[...], sh_ref[0], sc_ref[0]).astype(BF16)
        acc_ref[...] = jnp.zeros_like(acc_ref)

    h = jnp.maximum(_dot(u_ref[...], w1_ref[...]), 0.0)
    acc_ref[...] += _dot((h * h).astype(BF16), w2_ref[...])

    @pl.when(k == pl.num_programs(1) - 1)
    def _():
        o_ref[...] = x_ref[...] + gate_ref[0] * acc_ref[...]


def _mlp(x, g, mods, w1, w2, dims, rows, tm, tk):
    d = dims[0]
    hidden = w1.shape[1]
    return pl.pallas_call(
        _mlp_kernel,
        out_shape=jax.ShapeDtypeStruct((rows, d), F32),
        grid=(rows // tm, hidden // tk),
        in_specs=[pl.BlockSpec((tm, d), lambda i, k: (i, 0)),
                  pl.BlockSpec((1, d), lambda i, k: (0, 0)),
                  _mod_spec(dims, tm, 3, 2),
                  _mod_spec(dims, tm, 4, 2),
                  _mod_spec(dims, tm, 5, 2),
                  pl.BlockSpec((d, tk), lambda i, k: (0, k)),
                  pl.BlockSpec((tk, d), lambda i, k: (k, 0))],
        out_specs=pl.BlockSpec((tm, d), lambda i, k: (i, 0)),
        scratch_shapes=[pltpu.VMEM((tm, d), BF16), pltpu.VMEM((tm, d), F32)],
        compiler_params=_cparams(("parallel", "arbitrary")),
        name="mlp",
    )(x, g.reshape(1, d), mods, mods, mods, w1, w2)


def _proj_res_kernel(a_ref, res_ref, gate_ref, w_ref, o_ref):
    o_ref[...] = res_ref[...] + gate_ref[0] * _dot(a_ref[...], w_ref[...])


def _proj_res(a, res, mods, w, dims, rows, tm):
    d = dims[0]
    kdim = a.shape[1]
    return pl.pallas_call(
        _proj_res_kernel,
        out_shape=jax.ShapeDtypeStruct((rows, d), F32),
        grid=(rows // tm,),
        in_specs=[pl.BlockSpec((tm, kdim), lambda i: (i, 0)),
                  pl.BlockSpec((tm, d), lambda i: (i, 0)),
                  _mod_spec(dims, tm, 2, 1),
                  pl.BlockSpec((kdim, d), lambda i: (0, 0))],
        out_specs=pl.BlockSpec((tm, d), lambda i: (i, 0)),
        compiler_params=_cparams(("parallel",)),
        name="proj_res",
    )(a, res, mods, w)


def _rope(x, cos, sin):
    lane = lax.broadcasted_iota(jnp.int32, x.shape, 1)
    partner = jnp.where(lane % 32 < 16, pltpu.roll(x, LANES - 16, 1), pltpu.roll(x, 16, 1))
    return x * cos + partner * sin


def _mla_up_kernel(c_ref, qn_ref, kvn_ref, wqn_ref, wqr_ref, wkn_ref, wv_ref, cos_ref, sin_ref,
                   q_ref, k_ref, v_ref):
    c = c_ref[...]
    cq = c[:, :MLA_Q_RANK]
    ckv = c[:, MLA_Q_RANK:MLA_Q_RANK + MLA_KV_RANK]
    kr = c[:, MLA_Q_RANK + MLA_KV_RANK:]
    cq = (cq * lax.rsqrt(jnp.mean(cq * cq, -1, keepdims=True) + EPS) * qn_ref[...]).astype(BF16)
    ckv = (ckv * lax.rsqrt(jnp.mean(ckv * ckv, -1, keepdims=True) + EPS) * kvn_ref[...]).astype(BF16)
    cos = cos_ref[...]
    sin = sin_ref[...]
    scale = (MLA_NOPE + MLA_ROPE) ** -0.5
    qn = _dot(cq, wqn_ref[...]) * scale
    qr = _dot(cq, wqr_ref[...]) * scale
    kn = _dot(ckv, wkn_ref[...])
    v_ref[...] = _dot(ckv, wv_ref[...]).astype(v_ref.dtype)
    kr = _rope(kr, cos, sin).astype(k_ref.dtype)
    for h in range(MLA_HEADS):
        lo = h * MLA_QK_PAD
        sl = slice(h * LANES, (h + 1) * LANES)
        q_ref[:, lo:lo + LANES] = qn[:, sl].astype(q_ref.dtype)
        q_ref[:, lo + LANES:lo + 2 * LANES] = _rope(qr[:, sl], cos, sin).astype(q_ref.dtype)
        k_ref[:, lo:lo + LANES] = kn[:, sl].astype(k_ref.dtype)
        k_ref[:, lo + LANES:lo + 2 * LANES] = kr


def _mla_up(c, q_norm, kv_norm, wqn, wqr, wkn, wv, cos, sin, rows, tm):
    cw = c.shape[1]
    hq = MLA_HEADS * MLA_QK_PAD
    hv = MLA_HEADS * MLA_V
    full = lambda a: pl.BlockSpec(a.shape, lambda i: (0, 0))
    return pl.pallas_call(
        _mla_up_kernel,
        out_shape=(jax.ShapeDtypeStruct((rows, hq), BF16),
                   jax.ShapeDtypeStruct((rows, hq), BF16),
                   jax.ShapeDtypeStruct((rows, hv), BF16)),
        grid=(rows // tm,),
        in_specs=[pl.BlockSpec((tm, cw), lambda i: (i, 0)),
                  full(q_norm), full(kv_norm), full(wqn), full(wqr), full(wkn), full(wv),
                  pl.BlockSpec((tm, LANES), lambda i: (i, 0)),
                  pl.BlockSpec((tm, LANES), lambda i: (i, 0))],
        out_specs=(pl.BlockSpec((tm, hq), lambda i: (i, 0)),
                   pl.BlockSpec((tm, hq), lambda i: (i, 0)),
                   pl.BlockSpec((tm, hv), lambda i: (i, 0))),
        compiler_params=_cparams(("parallel",)),
        name="mla_up",
    )(c, q_norm, kv_norm, wqn, wqr, wkn, wv, cos, sin)


def _attn_kernel(*refs, n_lat, tk):
    if n_lat:
        q_ref, kl_ref, vl_ref, kc_ref, vc_ref, o_ref = refs
    else:
        q_ref, kc_ref, vc_ref, o_ref = refs
    q = q_ref[...]
    tq = q.shape[0]

    def step(k, v, carry):
        m, l, acc = carry
        s = _dot_nt(q, k)
        m_new = jnp.maximum(m, jnp.max(s, axis=-1, keepdims=True))
        alpha = jnp.exp(m - m_new)
        p = jnp.exp(s - m_new)
        l = alpha * l + jnp.sum(p, axis=-1, keepdims=True)
        acc = alpha * acc + _dot(p.astype(BF16), v)
        return m_new, l, acc

    carry = (jnp.full((tq, 1), -jnp.inf, F32), jnp.zeros((tq, 1), F32), jnp.zeros((tq, MLA_V), F32))
    if n_lat:
        def body(i, carry):
            off = pl.multiple_of(i * tk, tk)
            return step(kl_ref[pl.ds(off, tk), :], vl_ref[pl.ds(off, tk), :], carry)

        carry = lax.fori_loop(0, n_lat // tk, body, carry)
    m, l, acc = step(kc_ref[...], vc_ref[...], carry)
    o_ref[...] = (acc / l).astype(o_ref.dtype)


def _attention(q, k, v, dims, n_ctx, tq, tk, o_prev=None):
    d, seq, batch = dims
    rows = q.shape[0]
    ctx_blk0 = batch * seq // n_ctx
    kc_spec = pl.BlockSpec((n_ctx, MLA_QK_PAD), lambda b, h, i: (ctx_blk0 + b, h))
    vc_spec = pl.BlockSpec((n_ctx, MLA_V), lambda b, h, i: (ctx_blk0 + b, h))
    out_shape = jax.ShapeDtypeStruct((rows, MLA_HEADS * MLA_V), BF16)
    if o_prev is None:
        nq = seq // tq
        return pl.pallas_call(
            functools.partial(_attn_kernel, n_lat=seq, tk=tk),
            out_shape=out_shape,
            grid=(batch, MLA_HEADS, nq),
            in_specs=[pl.BlockSpec((tq, MLA_QK_PAD), lambda b, h, i: (b * nq + i, h)),
                      pl.BlockSpec((seq, MLA_QK_PAD), lambda b, h, i: (b, h)),
                      pl.BlockSpec((seq, MLA_V), lambda b, h, i: (b, h)),
                      kc_spec, vc_spec],
            out_specs=pl.BlockSpec((tq, MLA_V), lambda b, h, i: (b * nq + i, h)),
            compiler_params=_cparams(("parallel", "parallel", "arbitrary")),
            name="mla_attention",
        )(q, k, v, k, v)
    return pl.pallas_call(
        lambda q_ref, kc_ref, vc_ref, prev_ref, o_ref: _attn_kernel(q_ref, kc_ref, vc_ref, o_ref, n_lat=0, tk=tk),
        out_shape=out_shape,
        grid=(batch, MLA_HEADS, 1),
        in_specs=[pl.BlockSpec((n_ctx, MLA_QK_PAD), lambda b, h, i: (ctx_blk0 + b, h)),
                  kc_spec, vc_spec,
                  pl.BlockSpec(memory_space=pl.ANY)],
        out_specs=pl.BlockSpec((n_ctx, MLA_V), lambda b, h, i: (ctx_blk0 + b, h)),
        input_output_aliases={3: 0},
        compiler_params=_cparams(("parallel", "parallel", "arbitrary")),
        name="mla_attention_ctx",
    )(q, k, v, o_prev)


def _rope_tables(seq):
    t = jnp.arange(seq)
    row = (t // GRID_W).astype(F32)
    col = (t % GRID_W).astype(F32)
    axis_dim = MLA_ROPE // 2
    freqs = jnp.power(ROPE_THETA, -jnp.arange(0, axis_dim, 2, dtype=F32) / axis_dim)
    ar = row[:, None] * freqs[None, :]
    ac = col[:, None] * freqs[None, :]
    cr, sr, cc, sc = jnp.cos(ar), jnp.sin(ar), jnp.cos(ac), jnp.sin(ac)
    pad1 = jnp.ones((seq, LANES - MLA_ROPE), F32)
    pad0 = jnp.zeros((seq, LANES - MLA_ROPE), F32)
    cos = jnp.concatenate([cr, cr, cc, cc, pad1], -1)
    sin = jnp.concatenate([-sr, sr, -sc, sc, pad0], -1)
    return cos, sin


def _mla_layer(hs, mods, w, dims, n_ctx, ctx_out, rope):
    d, seq, batch = dims
    rows = hs.shape[0]
    lat_rows = batch * seq
    norm1, w_in, q_norm, kv_norm, w_q_up, w_kv_up, w_o = w
    c_w = MLA_Q_RANK + MLA_KV_RANK
    w_in_p = jnp.pad(w_in, ((0, 0), (0, LANES - MLA_ROPE))).astype(BF16)
    c = _nm_matmul(hs, norm1, mods, (0, 1), w_in_p, dims, rows, 256, w_in_p.shape[1], name="mla_w_in")
    wq = w_q_up.reshape(MLA_Q_RANK, MLA_HEADS, MLA_NOPE + MLA_ROPE)
    wqn = wq[:, :, :MLA_NOPE].reshape(MLA_Q_RANK, -1).astype(BF16)
    wqr = jnp.pad(wq[:, :, MLA_NOPE:], ((0, 0), (0, 0), (0, LANES - MLA_ROPE))).reshape(MLA_Q_RANK, -1).astype(BF16)
    wkv = w_kv_up.reshape(MLA_KV_RANK, MLA_HEADS, MLA_NOPE + MLA_V)
    wkn = wkv[:, :, :MLA_NOPE].reshape(MLA_KV_RANK, -1).astype(BF16)
    wv = wkv[:, :, MLA_NOPE:].reshape(MLA_KV_RANK, -1).astype(BF16)
    cos, sin = rope
    q, k, v = _mla_up(c, q_norm.reshape(1, -1), kv_norm.reshape(1, -1), wqn, wqr, wkn, wv, cos, sin, rows, 256)
    tq = min(512, seq)
    tk = min(512, seq)
    o = _attention(q, k, v, dims, n_ctx, tq, tk)
    out_rows = rows
    if ctx_out:
        o = _attention(q, k, v, dims, n_ctx, tq, tk, o_prev=o)
    else:
        out_rows = lat_rows
    return _proj_res(o, hs, mods, w_o.astype(BF16), dims, out_rows, 256)


def _seq_edges(i, tm, seq, batch, n_ctx):
    r0 = i * tm
    lat = r0 < batch * seq
    rel = jnp.where(lat, r0 % seq, (r0 - batch * seq) % n_ctx)
    length = jnp.where(lat, seq, n_ctx)
    return rel == 0, rel + tm == length


def _gdn_conv_kernel(prev_ref, cur_ref, next_ref, w_ref, o_ref, ext_ref, *, tm, seq, batch, n_ctx, n_norm):
    i = pl.program_id(0)
    j = pl.program_id(1)
    first, last = _seq_edges(i, tm, seq, batch, n_ctx)
    ext_ref[0:SUBLANES, :] = jnp.where(first, 0.0, prev_ref[...])
    ext_ref[SUBLANES:SUBLANES + tm, :] = cur_ref[...]
    ext_ref[SUBLANES + tm:, :] = jnp.where(last, 0.0, next_ref[...])
    pad = GDN_CONV // 2
    y = None
    for t in range(GDN_CONV):
        term = w_ref[t:t + 1, :] * ext_ref[pl.ds(SUBLANES - pad + t, tm), :]
        y = term if y is None else y + term
    y = y * _sigmoid(y)

    @pl.when(j >= n_norm)
    def _():
        o_ref[...] = y

    @pl.when(j < n_norm)
    def _():
        sc = jnp.where(j < n_norm // 2, GDN_DK ** -0.5, 1.0)
        for h in range(y.shape[1] // GDN_DK):
            seg = y[:, h * GDN_DK:(h + 1) * GDN_DK]
            o_ref[:, h * GDN_DK:(h + 1) * GDN_DK] = seg * (lax.rsqrt(jnp.sum(seg * seg, -1, keepdims=True) + EPS) * sc)


def _gdn_conv(qkvz, conv_w, dims, n_ctx, rows, tm, tc):
    d, seq, batch = dims
    chans = conv_w.shape[1]
    n_norm = 2 * GDN_K_HEADS * GDN_DK // tc
    rb = tm // SUBLANES
    last_blk = rows // SUBLANES - 1
    return pl.pallas_call(
        functools.partial(_gdn_conv_kernel, tm=tm, seq=seq, batch=batch, n_ctx=n_ctx, n_norm=n_norm),
        out_shape=jax.ShapeDtypeStruct((rows, chans), F32),
        grid=(rows // tm, chans // tc),
        in_specs=[pl.BlockSpec((SUBLANES, tc), lambda i, j: (jnp.maximum(i * rb - 1, 0), j)),
                  pl.BlockSpec((tm, tc), lambda i, j: (i, j)),
                  pl.BlockSpec((SUBLANES, tc), lambda i, j: (jnp.minimum((i + 1) * rb, last_blk), j)),
                  pl.BlockSpec((GDN_CONV, tc), lambda i, j: (0, j))],
        out_specs=pl.BlockSpec((tm, tc), lambda i, j: (i, j)),
        scratch_shapes=[pltpu.VMEM((tm + 2 * SUBLANES, tc), F32)],
        compiler_params=_cparams(("parallel", "arbitrary")),
        name="gdn_conv",
    )(qkvz, qkvz, qkvz, conv_w)


def _gdn_gate_kernel(x_ref, a_ref, dt_ref, o_ref):
    x = x_ref[...]
    lane = lax.broadcasted_iota(jnp.int32, x.shape, 1)
    g = -jnp.exp(a_ref[...]) * _softplus(x + dt_ref[...])
    o_ref[...] = jnp.where(lane % (2 * GDN_V_HEADS) >= GDN_V_HEADS, g, _sigmoid(x))


def _gdn_gates(gates, a_log, dt_bias, rows, tm):
    zeros = jnp.zeros_like(a_log)
    a_row = jnp.stack([zeros, a_log], 1).reshape(1, LANES)
    dt_row = jnp.stack([zeros, dt_bias], 1).reshape(1, LANES)
    return pl.pallas_call(
        _gdn_gate_kernel,
        out_shape=jax.ShapeDtypeStruct((rows, LANES), F32),
        grid=(rows // tm,),
        in_specs=[pl.BlockSpec((tm, LANES), lambda i: (i, 0)),
                  pl.BlockSpec((1, LANES), lambda i: (0, 0)),
                  pl.BlockSpec((1, LANES), lambda i: (0, 0))],
        out_specs=pl.BlockSpec((tm, LANES), lambda i: (i, 0)),
        compiler_params=_cparams(("parallel",)),
        name="gdn_gates",
    )(gates, a_row, dt_row)


def _chunk_masks(d):
    row = lax.broadcasted_iota(jnp.int32, (CHUNK, 2 * CHUNK), 0)
    col = lax.broadcasted_iota(jnp.int32, (CHUNK, 2 * CHUNK), 1)
    rel = (row - col % CHUNK) * (1 - 2 * d)
    return rel >= 0, rel > 0, col < CHUNK, row == col - CHUNK


def _cumsum_tables(x, d):
    r2 = lax.broadcasted_iota(jnp.int32, (2 * CHUNK, CHUNK), 0) % CHUNK
    c2 = lax.broadcasted_iota(jnp.int32, (2 * CHUNK, CHUNK), 1)
    incl2 = jnp.where((r2 - c2) * (1 - 2 * d) >= 0, 1.0, 0.0)
    cs2 = jnp.dot(incl2, x, preferred_element_type=F32, precision=lax.Precision.HIGHEST)
    return cs2[:CHUNK], cs2.T


def _gdn_scan_kernel(q_ref, k_ref, v_ref, beta_ref, g_ref, o_ref, s_ref):
    d = pl.program_id(0)

    @pl.when(pl.program_id(3) == 0)
    def _():
        s_ref[...] = jnp.zeros_like(s_ref)

    incl, strict, left, eye_right = _chunk_masks(d)
    g_all = g_ref[...]
    gc, gct = _cumsum_tables(g_all, d)
    g_tot = jnp.sum(g_all, axis=0, keepdims=True)
    beta_all = beta_ref[...]
    zeros_cd = jnp.zeros((CHUNK, GDN_DV), BF16)
    for kh in range(GDN_GROUP):
        qh = q_ref[:, kh * GDN_DK:(kh + 1) * GDN_DK]
        kk = k_ref[:, kh * GDN_DK:(kh + 1) * GDN_DK]
        kk_b = kk.astype(BF16)
        kk2 = jnp.concatenate([kk_b, kk_b], axis=0)
        kkw = _dot_nt(kk_b, kk2)
        qkw = _dot_nt(qh.astype(BF16), kk2)
        for r in range(GDN_V_HEADS // GDN_K_HEADS):
            h = kh * (GDN_V_HEADS // GDN_K_HEADS) + r
            vh = v_ref[:, h * GDN_DV:(h + 1) * GDN_DV]
            beta = beta_all[:, h:h + 1]
            gcol = gc[:, h:h + 1]
            grow = gct[h:h + 1, :]
            decay = jnp.exp(jnp.where(incl, gcol - grow, -jnp.inf))
            egc = jnp.exp(gcol)
            a_last = jnp.exp(g_tot[:, h:h + 1])
            n_mat = -jnp.where(strict, beta * kkw * decay, 0.0)
            ps = jnp.where(left, n_mat, jnp.where(eye_right, 1.0, 0.0))
            for _ in range(int(math.log2(CHUNK))):
                ps_b = ps.astype(BF16)
                rhs = jnp.concatenate([ps_b, jnp.zeros_like(ps_b)], axis=0)
                ps = _dot(ps_b, rhs) + jnp.where(left, 0.0, ps)
            kb = kk * beta
            rhs = jnp.concatenate([(vh * beta).astype(BF16), (kb * egc).astype(BF16)], axis=1)
            rhs = jnp.concatenate([jnp.zeros_like(rhs), rhs], axis=0)
            w = _dot(ps.astype(BF16), rhs)
            value = w[:, :GDN_DV]
            k_cum = w[:, GDN_DV:]
            attn = jnp.where(left, qkw * decay, 0.0).astype(BF16)
            q_dec = qh * egc
            k_tail = kk * jnp.exp(g_tot[:, h:h + 1] - gcol)
            state = s_ref[h]
            lhs = jnp.concatenate([k_cum.astype(BF16), q_dec.astype(BF16)], axis=0)
            ks_qs = _dot(lhs, state.astype(BF16))
            v_new = value - ks_qs[:CHUNK]
            v_new_b = v_new.astype(BF16)
            o = ks_qs[CHUNK:] + _dot(attn, jnp.concatenate([v_new_b, zeros_cd], axis=0))
            o_ref[:, h * GDN_DV:(h + 1) * GDN_DV] = o
            s_ref[h] = state * a_last + _dot_tn(k_tail.astype(BF16), v_new_b)


def _scan_row_block(d, b, s, seq, batch, n_ctx):
    nc = n_ctx // CHUNK
    nl = seq // CHUNK
    ctx_pos = jnp.where(d == 0, s, nc - 1 - s)
    lat_pos = jnp.where(d == 0, s - nc, nl - 1 - (s - nc))
    return jnp.where(s < nc, (batch * seq + b * n_ctx) // CHUNK + ctx_pos, b * nl + lat_pos)


def _group_gates(x, n_heads, group, rows):
    ng = n_heads // group
    x = x.reshape(rows, 2, 2, ng, group).transpose(2, 1, 3, 0, 4)
    x = jnp.pad(x, ((0, 0), (0, 0), (0, 0), (0, 0), (0, LANES - group)))
    return x[0], x[1]


def _gdn_scan(act, beta, g, dims, n_ctx, rows):
    d_model, seq, batch = dims
    rep = GDN_V_HEADS // GDN_K_HEADS
    ng = GDN_K_HEADS // GDN_GROUP
    gw = GDN_GROUP * GDN_DK
    vw = GDN_GROUP * rep * GDN_DV
    k_blk0 = GDN_K_HEADS * GDN_DK // gw
    v_blk0 = 2 * GDN_K_HEADS * GDN_DK // vw
    steps = (n_ctx + seq) // CHUNK
    rb = functools.partial(_scan_row_block, seq=seq, batch=batch, n_ctx=n_ctx)
    return pl.pallas_call(
        _gdn_scan_kernel,
        out_shape=jax.ShapeDtypeStruct((2, rows, GDN_V_HEADS * GDN_DV), F32),
        grid=(2, batch, ng, steps),
        in_specs=[pl.BlockSpec((CHUNK, gw), lambda d, b, hg, s: (rb(d, b, s), hg)),
                  pl.BlockSpec((CHUNK, gw), lambda d, b, hg, s: (rb(d, b, s), k_blk0 + hg)),
                  pl.BlockSpec((CHUNK, vw), lambda d, b, hg, s: (rb(d, b, s), v_blk0 + hg)),
                  pl.BlockSpec((None, None, CHUNK, LANES), lambda d, b, hg, s: (d, hg, rb(d, b, s), 0)),
                  pl.BlockSpec((None, None, CHUNK, LANES), lambda d, b, hg, s: (d, hg, rb(d, b, s), 0))],
        out_specs=pl.BlockSpec((None, CHUNK, vw), lambda d, b, hg, s: (d, rb(d, b, s), hg)),
        scratch_shapes=[pltpu.VMEM((GDN_GROUP * rep, GDN_DK, GDN_DV), F32)],
        compiler_params=_cparams(("parallel", "parallel", "parallel", "arbitrary")),
        name="gdn_scan",
    )(act, act, act, beta, g)


def _gdn_out_kernel(o_ref, z_ref, res_ref, gn_ref, gate_ref, w_ref, out_ref, acc_ref):
    k = pl.program_id(1)

    @pl.when(k == 0)
    def _():
        acc_ref[...] = jnp.zeros_like(acc_ref)

    o = o_ref[0] + o_ref[1]
    z = z_ref[...]
    gn = gn_ref[...]
    parts = []
    for h in range(o.shape[1] // GDN_DV):
        seg = o[:, h * GDN_DV:(h + 1) * GDN_DV]
        zz = z[:, h * GDN_DV:(h + 1) * GDN_DV]
        seg = seg * lax.rsqrt(jnp.mean(seg * seg, -1, keepdims=True) + EPS) * gn
        parts.append((seg * (zz * _sigmoid(zz))).astype(BF16))
    acc_ref[...] += _dot(jnp.concatenate(parts, axis=1), w_ref[...])

    @pl.when(k == pl.num_programs(1) - 1)
    def _():
        out_ref[...] = res_ref[...] + gate_ref[0] * acc_ref[...]


def _gdn_out(o, qkvz, res, out_norm, mods, w_o, dims, rows, tm, tkk):
    d = dims[0]
    width = o.shape[2]
    z_blk0 = (qkvz.shape[1] - width) // tkk
    return pl.pallas_call(
        _gdn_out_kernel,
        out_shape=jax.ShapeDtypeStruct((rows, d), F32),
        grid=(rows // tm, width // tkk),
        in_specs=[pl.BlockSpec((2, tm, tkk), lambda i, k: (0, i, k)),
                  pl.BlockSpec((tm, tkk), lambda i, k: (i, z_blk0 + k)),
                  pl.BlockSpec((tm, d), lambda i, k: (i, 0)),
                  pl.BlockSpec((1, GDN_DV), lambda i, k: (0, 0)),
                  _mod_spec(dims, tm, 2, 2),
                  pl.BlockSpec((tkk, d), lambda i, k: (k, 0))],
        out_specs=pl.BlockSpec((tm, d), lambda i, k: (i, 0)),
        scratch_shapes=[pltpu.VMEM((tm, d), F32)],
        compiler_params=_cparams(("parallel", "arbitrary")),
        name="gdn_out",
    )(o, qkvz, res, out_norm.reshape(1, GDN_DV), mods, w_o)


def _gdn_layer(hs, mods, w, dims, n_ctx):
    d, seq, batch = dims
    rows = hs.shape[0]
    norm1, w_in, conv_w, a_log, dt_bias, out_norm, w_o = w
    main_w = 2 * GDN_K_HEADS * GDN_DK + 2 * GDN_V_HEADS * GDN_DV
    qkvz = _nm_matmul(hs, norm1, mods, (0, 1), w_in[:, :main_w].astype(BF16), dims, rows, 512, 1024, name="gdn_w_in")
    gates = _nm_matmul(hs, norm1, mods, (0, 1), w_in[:, main_w:].astype(BF16), dims, rows, 512, LANES,
                       name="gdn_w_gates")
    act = _gdn_conv(qkvz, conv_w, dims, n_ctx, rows, 256, 1024)
    bg = _gdn_gates(gates, a_log, dt_bias, rows, 512)
    beta, g = _group_gates(bg, GDN_V_HEADS, GDN_GROUP * GDN_V_HEADS // GDN_K_HEADS, rows)
    o = _gdn_scan(act, beta, g, dims, n_ctx, rows)
    return _gdn_out(o, qkvz, hs, out_norm, mods, w_o.astype(BF16), dims, rows, 256, 1024)


def _ml_gate_kernel(x_ref, b_ref, o_ref):
    x = x_ref[...] + b_ref[...]
    lane = lax.broadcasted_iota(jnp.int32, x.shape, 1)
    o_ref[...] = jnp.where(lane % (2 * ML_HEADS) >= ML_HEADS, _log_sigmoid(x), x)


def _ml_gates(gates, gate_b, rows, tm):
    b_row = jnp.pad(gate_b.reshape(1, -1), ((0, 0), (0, LANES - 4 * ML_HEADS)))
    return pl.pallas_call(
        _ml_gate_kernel,
        out_shape=jax.ShapeDtypeStruct((rows, LANES), F32),
        grid=(rows // tm,),
        in_specs=[pl.BlockSpec((tm, LANES), lambda i: (i, 0)),
                  pl.BlockSpec((1, LANES), lambda i: (0, 0))],
        out_specs=pl.BlockSpec((tm, LANES), lambda i: (i, 0)),
        compiler_params=_cparams(("parallel",)),
        name="mlstm_gates",
    )(gates, b_row)


def _ml_scan_kernel(q_ref, k_ref, v_ref, li_ref, lf_ref, o_ref, c_ref, n_ref, m_ref):
    d = pl.program_id(0)

    @pl.when(pl.program_id(3) == 0)
    def _():
        c_ref[...] = jnp.zeros_like(c_ref)
        n_ref[...] = jnp.zeros_like(n_ref)
        m_ref[...] = jnp.zeros_like(m_ref)

    row = lax.broadcasted_iota(jnp.int32, (CHUNK, CHUNK), 0)
    col = lax.broadcasted_iota(jnp.int32, (CHUNK, CHUNK), 1)
    incl = (row - col) * (1 - 2 * d) >= 0
    lf = lf_ref[...]
    li = li_ref[...]
    bc, bct = _cumsum_tables(lf, d)
    b_tot = jnp.sum(lf, axis=0, keepdims=True)
    lit = jnp.concatenate([li, li], axis=0).T
    for h in range(ML_GROUP):
        qh = q_ref[:, h * ML_DQK:(h + 1) * ML_DQK]
        kh = k_ref[:, h * ML_DQK:(h + 1) * ML_DQK] * (ML_DQK ** -0.5)
        vh = v_ref[:, h * ML_DV:(h + 1) * ML_DV].astype(BF16)
        bcol = bc[:, h:h + 1]
        brow = bct[h:h + 1, :CHUNK]
        icol = li[:, h:h + 1]
        irow = lit[h:h + 1, :CHUNK]
        b_last = b_tot[:, h:h + 1]
        m = m_ref[h][0:1, 0:1]
        d_mat = jnp.where(incl, bcol - brow + irow, -jnp.inf)
        d_max = jnp.max(d_mat, axis=-1, keepdims=True)
        qh_b = qh.astype(BF16)
        qk = _dot_nt(qh_b, kh.astype(BF16))
        w_end = b_last - bcol + icol
        w_end_max = jnp.max(w_end, axis=0, keepdims=True)
        inter = bcol + m
        mt = jnp.maximum(d_max, inter)
        p = jnp.exp(d_mat - mt) * qk
        a_in = jnp.exp(inter - mt)
        c_mat = c_ref[h]
        n_vec = n_ref[h]
        num = _dot(p.astype(BF16), vh) + a_in * _dot(qh_b, c_mat.astype(BF16))
        den = jnp.sum(p, axis=-1, keepdims=True) + a_in * jnp.sum(qh * n_vec, axis=-1, keepdims=True)
        o_ref[:, h * ML_DV:(h + 1) * ML_DV] = num / jnp.maximum(jnp.abs(den), jnp.exp(-mt))
        m_new = jnp.maximum(b_last + m, w_end_max)
        carry_decay = jnp.exp(b_last + m - m_new)
        kw = kh * jnp.exp(w_end - m_new)
        c_ref[h] = carry_decay * c_mat + _dot_tn(kw.astype(BF16), vh)
        n_ref[h] = carry_decay * n_vec + jnp.sum(kw, axis=0, keepdims=True)
        m_ref[h] = jnp.broadcast_to(m_new, m_ref.shape[1:])


def _ml_scan(qkvo, li, lf, dims, n_ctx, rows):
    d_model, seq, batch = dims
    ng = ML_HEADS // ML_GROUP
    qw = ML_GROUP * ML_DQK
    vw = ML_GROUP * ML_DV
    k_blk0 = ML_HEADS * ML_DQK // qw
    v_blk0 = 2 * ML_HEADS * ML_DQK // vw
    steps = (n_ctx + seq) // CHUNK
    rb = functools.partial(_scan_row_block, seq=seq, batch=batch, n_ctx=n_ctx)
    return pl.pallas_call(
        _ml_scan_kernel,
        out_shape=jax.ShapeDtypeStruct((2, rows, ML_HEADS * ML_DV), F32),
        grid=(2, batch, ng, steps),
        in_specs=[pl.BlockSpec((CHUNK, qw), lambda d, b, hg, s: (rb(d, b, s), hg)),
                  pl.BlockSpec((CHUNK, qw), lambda d, b, hg, s: (rb(d, b, s), k_blk0 + hg)),
                  pl.BlockSpec((CHUNK, vw), lambda d, b, hg, s: (rb(d, b, s), v_blk0 + hg)),
                  pl.BlockSpec((None, None, CHUNK, LANES), lambda d, b, hg, s: (d, hg, rb(d, b, s), 0)),
                  pl.BlockSpec((None, None, CHUNK, LANES), lambda d, b, hg, s: (d, hg, rb(d, b, s), 0))],
        out_specs=pl.BlockSpec((None, CHUNK, vw), lambda d, b, hg, s: (d, rb(d, b, s), hg)),
        scratch_shapes=[pltpu.VMEM((ML_GROUP, ML_DQK, ML_DV), F32),
                        pltpu.VMEM((ML_GROUP, 1, ML_DQK), F32),
                        pltpu.VMEM((ML_GROUP, SUBLANES, LANES), F32)],
        compiler_params=_cparams(("parallel", "parallel", "parallel", "arbitrary")),
        name="mlstm_scan",
    )(qkvo, qkvo, qkvo, li, lf)


def _ml_out_kernel(h_ref, og_ref, res_ref, gn_ref, gate_ref, w_ref, out_ref):
    hs = h_ref[0] + h_ref[1]
    og = og_ref[...]
    gn = gn_ref[...]
    parts = []
    for h in range(ML_HEADS):
        sl = slice(h * ML_DV, (h + 1) * ML_DV)
        seg = hs[:, sl]
        seg = seg * lax.rsqrt(jnp.mean(seg * seg, -1, keepdims=True) + EPS) * gn[:, sl]
        parts.append((seg * _sigmoid(og[:, sl])).astype(BF16))
    out_ref[...] = res_ref[...] + gate_ref[0] * _dot(jnp.concatenate(parts, axis=1), w_ref[...])


def _ml_out(hh, qkvo, res, out_norm, mods, w_o, dims, rows, tm):
    d = dims[0]
    width = hh.shape[2]
    og_blk = (qkvo.shape[1] - width) // width
    return pl.pallas_call(
        _ml_out_kernel,
        out_shape=jax.ShapeDtypeStruct((rows, d), F32),
        grid=(rows // tm,),
        in_specs=[pl.BlockSpec((2, tm, width), lambda i: (0, i, 0)),
                  pl.BlockSpec((tm, width), lambda i: (i, og_blk)),
                  pl.BlockSpec((tm, d), lambda i: (i, 0)),
                  pl.BlockSpec((1, width), lambda i: (0, 0)),
                  _mod_spec(dims, tm, 2, 1),
                  pl.BlockSpec((width, d), lambda i: (0, 0))],
        out_specs=pl.BlockSpec((tm, d), lambda i: (i, 0)),
        compiler_params=_cparams(("parallel",)),
        name="mlstm_out",
    )(hh, qkvo, res, out_norm.reshape(1, width), mods, w_o)


def _mlstm_layer(hs, mods, w, dims, n_ctx):
    d, seq, batch = dims
    rows = hs.shape[0]
    norm1, w_in, gate_b, out_norm, w_o = w
    main_w = 2 * ML_HEADS * ML_DQK + 2 * ML_HEADS * ML_DV
    qkvo = _nm_matmul(hs, norm1, mods, (0, 1), w_in[:, :main_w].astype(BF16), dims, rows, 512, 1024,
                      name="mlstm_w_in")
    w_g = jnp.pad(w_in[:, main_w:], ((0, 0), (0, LANES - 4 * ML_HEADS))).astype(BF16)
    gates = _nm_matmul(hs, norm1, mods, (0, 1), w_g, dims, rows, 512, LANES, name="mlstm_w_gates")
    lg = _ml_gates(gates, gate_b, rows, 512)
    li, lf = _group_gates(lg[:, :4 * ML_HEADS], ML_HEADS, ML_GROUP, rows)
    hh = _ml_scan(qkvo, li, lf, dims, n_ctx, rows)
    return _ml_out(hh, qkvo, hs, out_norm, mods, w_o.astype(BF16), dims, rows, 256)


def _final_norm_kernel(x_ref, g_ref, o_ref):
    x = x_ref[...]
    o_ref[...] = x * lax.rsqrt(jnp.mean(x * x, -1, keepdims=True) + EPS) * g_ref[...]


def _final_norm(x, g, rows, tm):
    d = x.shape[1]
    return pl.pallas_call(
        _final_norm_kernel,
        out_shape=jax.ShapeDtypeStruct((rows, d), F32),
        grid=(rows // tm,),
        in_specs=[pl.BlockSpec((tm, d), lambda i: (i, 0)),
                  pl.BlockSpec((1, d), lambda i: (0, 0))],
        out_specs=pl.BlockSpec((tm, d), lambda i: (i, 0)),
        compiler_params=_cparams(("parallel",)),
        name="final_norm",
    )(x, g.reshape(1, d))


def kernel(x, c, ctx, c_ctx, l0_ada_w, l0_ada_b, l0_norm1, l0_mla_w_in, l0_mla_q_norm, l0_mla_kv_norm, l0_mla_w_q_up, l0_mla_w_kv_up, l0_mla_w_o, l0_norm2, l0_mlp_w1, l0_mlp_w2, l1_ada_w, l1_ada_b, l1_norm1, l1_gdn_w_in, l1_gdn_conv_w, l1_gdn_a_log, l1_gdn_dt_bias, l1_gdn_out_norm, l1_gdn_w_o, l1_norm2, l1_mlp_w1, l1_mlp_w2, l2_ada_w, l2_ada_b, l2_norm1, l2_mlstm_w_in, l2_mlstm_gate_b, l2_mlstm_out_norm, l2_mlstm_w_o, l2_norm2, l2_mlp_w1, l2_mlp_w2, l3_ada_w, l3_ada_b, l3_norm1, l3_mla_w_in, l3_mla_q_norm, l3_mla_kv_norm, l3_mla_w_q_up, l3_mla_w_kv_up, l3_mla_w_o, l3_norm2, l3_mlp_w1, l3_mlp_w2, final_norm):
    batch, seq, d = x.shape
    n_ctx = ctx.shape[1]
    dims = (d, seq, batch)
    lat_rows = batch * seq
    rows = lat_rows + batch * n_ctx
    layers = (
        ("mla", (l0_ada_w, l0_ada_b, l0_norm2, l0_mlp_w1, l0_mlp_w2),
         (l0_norm1, l0_mla_w_in, l0_mla_q_norm, l0_mla_kv_norm, l0_mla_w_q_up, l0_mla_w_kv_up, l0_mla_w_o)),
        ("gdn", (l1_ada_w, l1_ada_b, l1_norm2, l1_mlp_w1, l1_mlp_w2),
         (l1_norm1, l1_gdn_w_in, l1_gdn_conv_w, l1_gdn_a_log, l1_gdn_dt_bias, l1_gdn_out_norm, l1_gdn_w_o)),
        ("mlstm", (l2_ada_w, l2_ada_b, l2_norm2, l2_mlp_w1, l2_mlp_w2),
         (l2_norm1, l2_mlstm_w_in, l2_mlstm_gate_b, l2_mlstm_out_norm, l2_mlstm_w_o)),
        ("mla", (l3_ada_w, l3_ada_b, l3_norm2, l3_mlp_w1, l3_mlp_w2),
         (l3_norm1, l3_mla_w_in, l3_mla_q_norm, l3_mla_kv_norm, l3_mla_w_q_up, l3_mla_w_kv_up, l3_mla_w_o)),
    )
    cond = jnp.concatenate([c, c_ctx[None, :], jnp.zeros((SUBLANES - batch - 1, d), F32)], 0)
    cos, sin = _rope_tables(seq)
    ctx_pad = ((0, batch * n_ctx), (0, 0))
    rope = (jnp.pad(jnp.tile(cos, (batch, 1)), ctx_pad, constant_values=1.0),
            jnp.pad(jnp.tile(sin, (batch, 1)), ctx_pad))
    hs = jnp.concatenate([x.reshape(lat_rows, d), ctx.reshape(batch * n_ctx, d)], 0)
    for li, (kind, (ada_w, ada_b, norm2, mlp_w1, mlp_w2), mixer_w) in enumerate(layers):
        ctx_out = li < len(layers) - 1
        mods = _ada(cond, ada_w, ada_b)[:batch + 1].reshape(batch + 1, 1, N_MOD * d)
        if kind == "mla":
            hs = _mla_layer(hs, mods, mixer_w, dims, n_ctx, ctx_out, rope)
        elif kind == "gdn":
            hs = _gdn_layer(hs, mods, mixer_w, dims, n_ctx)
        else:
            hs = _mlstm_layer(hs, mods, mixer_w, dims, n_ctx)
        hs = _mlp(hs, norm2, mods, mlp_w1.astype(BF16), mlp_w2.astype(BF16), dims, hs.shape[0], 512, 512)
    return _final_norm(hs, final_norm, lat_rows, 512).reshape(batch, seq, d)
```

```python
import functools
import math

import jax
import jax.numpy as jnp
from jax import lax
from jax.experimental import pallas as pl
from jax.experimental.pallas import tpu as pltpu

F32 = jnp.float32
BF16 = jnp.bfloat16

EPS = 1e-6
ROPE_THETA = 10000.0
GRID_W = 64
N_MOD = 6

MLA_HEADS = 16
MLA_Q_RANK = 768
MLA_KV_RANK = 512
MLA_NOPE = 128
MLA_ROPE = 64
MLA_V = 128
MLA_QK_PAD = 256

GDN_K_HEADS = 16
GDN_V_HEADS = 32
GDN_DK = 128
GDN_DV = 128
GDN_CONV = 5
GDN_GROUP = 4
GDN_CHAIN = 4

ML_HEADS = 8
ML_DQK = 128
ML_DV = 256
ML_GROUP = 4

CHUNK = 64
LANES = 128
SUBLANES = 8

VMEM_LIMIT = 52 * 1024 * 1024


def _cparams(sem):
    return pltpu.CompilerParams(dimension_semantics=sem, vmem_limit_bytes=VMEM_LIMIT)


def _dot(a, b):
    return jnp.dot(a, b, preferred_element_type=F32)


def _dot_nt(a, b):
    return lax.dot_general(a, b, (((1,), (1,)), ((), ())), preferred_element_type=F32)


def _dot_tn(a, b):
    return lax.dot_general(a, b, (((0,), (0,)), ((), ())), preferred_element_type=F32)


def _normmod(x, g, shift, scale):
    y = x * lax.rsqrt(jnp.mean(x * x, axis=-1, keepdims=True) + EPS) * g
    return y * (1.0 + scale) + shift


def _sigmoid(x):
    return 1.0 / (1.0 + jnp.exp(-x))


def _softplus(x):
    return jnp.maximum(x, 0.0) + jnp.log(1.0 + jnp.exp(-jnp.abs(x)))


def _log_sigmoid(x):
    return -_softplus(-x)


def _ada_kernel(c_ref, w_ref, b_ref, o_ref):
    c = c_ref[...]
    s = (c * _sigmoid(c)).astype(BF16)
    o_ref[...] = _dot(s, w_ref[...].astype(BF16)) + b_ref[...]


def _ada(cond, w, b):
    rows, d = cond.shape
    n = w.shape[1]
    tn = 1024
    return pl.pallas_call(
        _ada_kernel,
        out_shape=jax.ShapeDtypeStruct((rows, n), F32),
        grid=(n // tn,),
        in_specs=[pl.BlockSpec((rows, d), lambda j: (0, 0)),
                  pl.BlockSpec((d, tn), lambda j: (0, j)),
                  pl.BlockSpec((1, tn), lambda j: (0, j))],
        out_specs=pl.BlockSpec((rows, tn), lambda j: (0, j)),
        compiler_params=_cparams(("arbitrary",)),
        name="ada_mod",
    )(cond, w, b.reshape(1, n))


def _nm_matmul_kernel(x_ref, g_ref, sh_ref, sc_ref, w_ref, o_ref, u_ref):
    @pl.when(pl.program_id(1) == 0)
    def _():
        u_ref[...] = _normmod(x_ref[...], g_ref[...], sh_ref[0], sc_ref[0]).astype(BF16)

    o_ref[...] = _dot(u_ref[...], w_ref[...]).astype(o_ref.dtype)


def _mod_spec(dims, tm, kind, ngrid):
    d, seq, batch = dims
    if ngrid == 1:
        return pl.BlockSpec((1, 1, d), lambda i: (jnp.minimum(i * tm // seq, batch), 0, kind))
    return pl.BlockSpec((1, 1, d), lambda i, j: (jnp.minimum(i * tm // seq, batch), 0, kind))


def _nm_matmul(x, g, mods, kinds, w, dims, rows, tm, tn, out_dtype=F32, name="nm_matmul"):
    d = dims[0]
    n = w.shape[1]
    return pl.pallas_call(
        _nm_matmul_kernel,
        out_shape=jax.ShapeDtypeStruct((rows, n), out_dtype),
        grid=(rows // tm, n // tn),
        in_specs=[pl.BlockSpec((tm, d), lambda i, j: (i, 0)),
                  pl.BlockSpec((1, d), lambda i, j: (0, 0)),
                  _mod_spec(dims, tm, kinds[0], 2),
                  _mod_spec(dims, tm, kinds[1], 2),
                  pl.BlockSpec((d, tn), lambda i, j: (0, j))],
        out_specs=pl.BlockSpec((tm, tn), lambda i, j: (i, j)),
        scratch_shapes=[pltpu.VMEM((tm, d), BF16)],
        compiler_params=_cparams(("parallel", "arbitrary")),
        name=name,
    )(x, g.reshape(1, d), mods, mods, w)


def _mlp_kernel(x_ref, g_ref, sh_ref, sc_ref, gate_ref, w1_ref, w2_ref, o_ref, u_ref, acc_ref):
    k = pl.program_id(1)

    @pl.when(k == 0)
    def _():
        u_ref[...] = _normmod(x_ref[...], g_ref[...], sh_ref[0], sc_ref[0]).astype(BF16)
        acc_ref[...] = jnp.zeros_like(acc_ref)

    h = jnp.maximum(_dot(u_ref[...], w1_ref[...]), 0.0)
    acc_ref[...] += _dot((h * h).astype(BF16), w2_ref[...])

    @pl.when(k == pl.num_programs(1) - 1)
    def _():
        o_ref[...] = x_ref[...] + gate_ref[0] * acc_ref[...]


def _mlp(x, g, mods, w1, w2, dims, rows, tm, tk):
    d = dims[0]
    hidden = w1.shape[1]
    return pl.pallas_call(
        _mlp_kernel,
        out_shape=jax.ShapeDtypeStruct((rows, d), F32),
        grid=(rows // tm, hidden // tk),
        in_specs=[pl.BlockSpec((tm, d), lambda i, k: (i, 0)),
                  pl.BlockSpec((1, d), lambda i, k: (0, 0)),
                  _mod_spec(dims, tm, 3, 2),
                  _mod_spec(dims, tm, 4, 2),
                  _mod_spec(dims, tm, 5, 2),
                  pl.BlockSpec((d, tk), lambda i, k: (0, k)),
                  pl.BlockSpec((tk, d), lambda i, k: (k, 0))],
        out_specs=pl.BlockSpec((tm, d), lambda i, k: (i, 0)),
        scratch_shapes=[pltpu.VMEM((tm, d), BF16), pltpu.VMEM((tm, d), F32)],
        compiler_params=_cparams(("parallel", "arbitrary")),
        name="mlp",
    )(x, g.reshape(1, d), mods, mods, mods, w1, w2)


def _proj_res_kernel(a_ref, res_ref, gate_ref, w_ref, o_ref):
    o_ref[...] = res_ref[...] + gate_ref[0] * _dot(a_ref[...], w_ref[...])


def _proj_res(a, res, mods, w, dims, rows, tm):
    d = dims[0]
    kdim = a.shape[1]
    return pl.pallas_call(
        _proj_res_kernel,
        out_shape=jax.ShapeDtypeStruct((rows, d), F32),
        grid=(rows // tm,),
        in_specs=[pl.BlockSpec((tm, kdim), lambda i: (i, 0)),
                  pl.BlockSpec((tm, d), lambda i: (i, 0)),
                  _mod_spec(dims, tm, 2, 1),
                  pl.BlockSpec((kdim, d), lambda i: (0, 0))],
        out_specs=pl.BlockSpec((tm, d), lambda i: (i, 0)),
        compiler_params=_cparams(("parallel",)),
        name="proj_res",
    )(a, res, mods, w)


def _rope(x, cos, sin):
    lane = lax.broadcasted_iota(jnp.int32, x.shape, 1)
    partner = jnp.where(lane % 32 < 16, pltpu.roll(x, LANES - 16, 1), pltpu.roll(x, 16, 1))
    return x * cos + partner * sin


def _mla_up_kernel(c_ref, qn_ref, kvn_ref, wqn_ref, wqr_ref, wkn_ref, wv_ref, cos_ref, sin_ref,
                   q_ref, k_ref, v_ref):
    c = c_ref[...]
    cq = c[:, :MLA_Q_RANK]
    ckv = c[:, MLA_Q_RANK:MLA_Q_RANK + MLA_KV_RANK]
    kr = c[:, MLA_Q_RANK + MLA_KV_RANK:]
    cq = (cq * lax.rsqrt(jnp.mean(cq * cq, -1, keepdims=True) + EPS) * qn_ref[...]).astype(BF16)
    ckv = (ckv * lax.rsqrt(jnp.mean(ckv * ckv, -1, keepdims=True) + EPS) * kvn_ref[...]).astype(BF16)
    cos = cos_ref[...]
    sin = sin_ref[...]
    scale = (MLA_NOPE + MLA_ROPE) ** -0.5
    qn = _dot(cq, wqn_ref[...]) * scale
    qr = _dot(cq, wqr_ref[...]) * scale
    kn = _dot(ckv, wkn_ref[...])
    v_ref[...] = _dot(ckv, wv_ref[...]).astype(v_ref.dtype)
    kr = _rope(kr, cos, sin).astype(k_ref.dtype)
    for h in range(MLA_HEADS):
        lo = h * MLA_QK_PAD
        sl = slice(h * LANES, (h + 1) * LANES)
        q_ref[:, lo:lo + LANES] = qn[:, sl].astype(q_ref.dtype)
        q_ref[:, lo + LANES:lo + 2 * LANES] = _rope(qr[:, sl], cos, sin).astype(q_ref.dtype)
        k_ref[:, lo:lo + LANES] = kn[:, sl].astype(k_ref.dtype)
        k_ref[:, lo + LANES:lo + 2 * LANES] = kr


def _mla_up(c, q_norm, kv_norm, wqn, wqr, wkn, wv, cos, sin, rows, tm):
    cw = c.shape[1]
    hq = MLA_HEADS * MLA_QK_PAD
    hv = MLA_HEADS * MLA_V
    full = lambda a: pl.BlockSpec(a.shape, lambda i: (0, 0))
    return pl.pallas_call(
        _mla_up_kernel,
        out_shape=(jax.ShapeDtypeStruct((rows, hq), BF16),
                   jax.ShapeDtypeStruct((rows, hq), BF16),
                   jax.ShapeDtypeStruct((rows, hv), BF16)),
        grid=(rows // tm,),
        in_specs=[pl.BlockSpec((tm, cw), lambda i: (i, 0)),
                  full(q_norm), full(kv_norm), full(wqn), full(wqr), full(wkn), full(wv),
                  pl.BlockSpec((tm, LANES), lambda i: (i, 0)),
                  pl.BlockSpec((tm, LANES), lambda i: (i, 0))],
        out_specs=(pl.BlockSpec((tm, hq), lambda i: (i, 0)),
                   pl.BlockSpec((tm, hq), lambda i: (i, 0)),
                   pl.BlockSpec((tm, hv), lambda i: (i, 0))),
        compiler_params=_cparams(("parallel",)),
        name="mla_up",
    )(c, q_norm, kv_norm, wqn, wqr, wkn, wv, cos, sin)


def _attn_kernel(*refs, n_lat, tk):
    if n_lat:
        q_ref, kl_ref, vl_ref, kc_ref, vc_ref, o_ref = refs
    else:
        q_ref, kc_ref, vc_ref, o_ref = refs
    q = q_ref[...]
    tq = q.shape[0]

    def step(k, v, carry):
        m, l, acc = carry
        s = _dot_nt(q, k)
        m_new = jnp.maximum(m, jnp.max(s, axis=-1, keepdims=True))
        alpha = jnp.exp(m - m_new)
        p = jnp.exp(s - m_new)
        l = alpha * l + jnp.sum(p, axis=-1, keepdims=True)
        acc = alpha * acc + _dot(p.astype(BF16), v)
        return m_new, l, acc

    carry = (jnp.full((tq, 1), -jnp.inf, F32), jnp.zeros((tq, 1), F32), jnp.zeros((tq, MLA_V), F32))
    if n_lat:
        def body(i, carry):
            off = pl.multiple_of(i * tk, tk)
            return step(kl_ref[pl.ds(off, tk), :], vl_ref[pl.ds(off, tk), :], carry)

        carry = lax.fori_loop(0, n_lat // tk, body, carry)
    m, l, acc = step(kc_ref[...], vc_ref[...], carry)
    o_ref[...] = (acc / l).astype(o_ref.dtype)


def _attention(q, k, v, dims, n_ctx, tq, tk, o_prev=None):
    d, seq, batch = dims
    rows = q.shape[0]
    ctx_blk0 = batch * seq // n_ctx
    kc_spec = pl.BlockSpec((n_ctx, MLA_QK_PAD), lambda b, h, i: (ctx_blk0 + b, h))
    vc_spec = pl.BlockSpec((n_ctx, MLA_V), lambda b, h, i: (ctx_blk0 + b, h))
    out_shape = jax.ShapeDtypeStruct((rows, MLA_HEADS * MLA_V), BF16)
    if o_prev is None:
        nq = seq // tq
        return pl.pallas_call(
            functools.partial(_attn_kernel, n_lat=seq, tk=tk),
            out_shape=out_shape,
            grid=(batch, MLA_HEADS, nq),
            in_specs=[pl.BlockSpec((tq, MLA_QK_PAD), lambda b, h, i: (b * nq + i, h)),
                      pl.BlockSpec((seq, MLA_QK_PAD), lambda b, h, i: (b, h)),
                      pl.BlockSpec((seq, MLA_V), lambda b, h, i: (b, h)),
                      kc_spec, vc_spec],
            out_specs=pl.BlockSpec((tq, MLA_V), lambda b, h, i: (b * nq + i, h)),
            compiler_params=_cparams(("parallel", "parallel", "arbitrary")),
            name="mla_attention",
        )(q, k, v, k, v)
    return pl.pallas_call(
        lambda q_ref, kc_ref, vc_ref, prev_ref, o_ref: _attn_kernel(q_ref, kc_ref, vc_ref, o_ref, n_lat=0, tk=tk),
        out_shape=out_shape,
        grid=(batch, MLA_HEADS, 1),
        in_specs=[pl.BlockSpec((n_ctx, MLA_QK_PAD), lambda b, h, i: (ctx_blk0 + b, h)),
                  kc_spec, vc_spec,
                  pl.BlockSpec(memory_space=pl.ANY)],
        out_specs=pl.BlockSpec((n_ctx, MLA_V), lambda b, h, i: (ctx_blk0 + b, h)),
        input_output_aliases={3: 0},
        compiler_params=_cparams(("parallel", "parallel", "arbitrary")),
        name="mla_attention_ctx",
    )(q, k, v, o_prev)


def _rope_tables(seq):
    t = jnp.arange(seq)
    row = (t // GRID_W).astype(F32)
    col = (t % GRID_W).astype(F32)
    axis_dim = MLA_ROPE // 2
    freqs = jnp.power(ROPE_THETA, -jnp.arange(0, axis_dim, 2, dtype=F32) / axis_dim)
    ar = row[:, None] * freqs[None, :]
    ac = col[:, None] * freqs[None, :]
    cr, sr, cc, sc = jnp.cos(ar), jnp.sin(ar), jnp.cos(ac), jnp.sin(ac)
    pad1 = jnp.ones((seq, LANES - MLA_ROPE), F32)
    pad0 = jnp.zeros((seq, LANES - MLA_ROPE), F32)
    cos = jnp.concatenate([cr, cr, cc, cc, pad1], -1)
    sin = jnp.concatenate([-sr, sr, -sc, sc, pad0], -1)
    return cos, sin


def _mla_layer(hs, mods, w, dims, n_ctx, ctx_out, rope):
    d, seq, batch = dims
    rows = hs.shape[0]
    lat_rows = batch * seq
    norm1, w_in, q_norm, kv_norm, w_q_up, w_kv_up, w_o = w
    c_w = MLA_Q_RANK + MLA_KV_RANK
    w_in_p = jnp.pad(w_in, ((0, 0), (0, LANES - MLA_ROPE))).astype(BF16)
    c = _nm_matmul(hs, norm1, mods, (0, 1), w_in_p, dims, rows, 256, w_in_p.shape[1], name="mla_w_in")
    wq = w_q_up.reshape(MLA_Q_RANK, MLA_HEADS, MLA_NOPE + MLA_ROPE)
    wqn = wq[:, :, :MLA_NOPE].reshape(MLA_Q_RANK, -1).astype(BF16)
    wqr = jnp.pad(wq[:, :, MLA_NOPE:], ((0, 0), (0, 0), (0, LANES - MLA_ROPE))).reshape(MLA_Q_RANK, -1).astype(BF16)
    wkv = w_kv_up.reshape(MLA_KV_RANK, MLA_HEADS, MLA_NOPE + MLA_V)
    wkn = wkv[:, :, :MLA_NOPE].reshape(MLA_KV_RANK, -1).astype(BF16)
    wv = wkv[:, :, MLA_NOPE:].reshape(MLA_KV_RANK, -1).astype(BF16)
    cos, sin = rope
    q, k, v = _mla_up(c, q_norm.reshape(1, -1), kv_norm.reshape(1, -1), wqn, wqr, wkn, wv, cos, sin, rows, 256)
    tq = min(512, seq)
    tk = min(512, seq)
    o = _attention(q, k, v, dims, n_ctx, tq, tk)
    out_rows = rows
    if ctx_out:
        o = _attention(q, k, v, dims, n_ctx, tq, tk, o_prev=o)
    else:
        out_rows = lat_rows
    return _proj_res(o, hs, mods, w_o.astype(BF16), dims, out_rows, 256)


def _seq_edges(i, tm, seq, batch, n_ctx):
    r0 = i * tm
    lat = r0 < batch * seq
    rel = jnp.where(lat, r0 % seq, (r0 - batch * seq) % n_ctx)
    length = jnp.where(lat, seq, n_ctx)
    return rel == 0, rel + tm == length


def _gdn_conv_kernel(prev_ref, cur_ref, next_ref, w_ref, o_ref, ext_ref, *, tm, seq, batch, n_ctx, n_norm):
    i = pl.program_id(0)
    j = pl.program_id(1)
    first, last = _seq_edges(i, tm, seq, batch, n_ctx)
    ext_ref[0:SUBLANES, :] = jnp.where(first, 0.0, prev_ref[...])
    ext_ref[SUBLANES:SUBLANES + tm, :] = cur_ref[...]
    ext_ref[SUBLANES + tm:, :] = jnp.where(last, 0.0, next_ref[...])
    pad = GDN_CONV // 2
    y = None
    for t in range(GDN_CONV):
        term = w_ref[t:t + 1, :] * ext_ref[pl.ds(SUBLANES - pad + t, tm), :]
        y = term if y is None else y + term
    y = y * _sigmoid(y)

    @pl.when(j >= n_norm)
    def _():
        o_ref[...] = y

    @pl.when(j < n_norm)
    def _():
        sc = jnp.where(j < n_norm // 2, GDN_DK ** -0.5, 1.0)
        for h in range(y.shape[1] // GDN_DK):
            seg = y[:, h * GDN_DK:(h + 1) * GDN_DK]
            o_ref[:, h * GDN_DK:(h + 1) * GDN_DK] = seg * (lax.rsqrt(jnp.sum(seg * seg, -1, keepdims=True) + EPS) * sc)


def _gdn_conv(qkvz, conv_w, dims, n_ctx, rows, tm, tc):
    d, seq, batch = dims
    chans = conv_w.shape[1]
    n_norm = 2 * GDN_K_HEADS * GDN_DK // tc
    rb = tm // SUBLANES
    last_blk = rows // SUBLANES - 1
    return pl.pallas_call(
        functools.partial(_gdn_conv_kernel, tm=tm, seq=seq, batch=batch, n_ctx=n_ctx, n_norm=n_norm),
        out_shape=jax.ShapeDtypeStruct((rows, chans), F32),
        grid=(rows // tm, chans // tc),
        in_specs=[pl.BlockSpec((SUBLANES, tc), lambda i, j: (jnp.maximum(i * rb - 1, 0), j)),
                  pl.BlockSpec((tm, tc), lambda i, j: (i, j)),
                  pl.BlockSpec((SUBLANES, tc), lambda i, j: (jnp.minimum((i + 1) * rb, last_blk), j)),
                  pl.BlockSpec((GDN_CONV, tc), lambda i, j: (0, j))],
        out_specs=pl.BlockSpec((tm, tc), lambda i, j: (i, j)),
        scratch_shapes=[pltpu.VMEM((tm + 2 * SUBLANES, tc), F32)],
        compiler_params=_cparams(("parallel", "arbitrary")),
        name="gdn_conv",
    )(qkvz, qkvz, qkvz, conv_w)


def _gdn_gate_kernel(x_ref, a_ref, dt_ref, o_ref):
    x = x_ref[...]
    lane = lax.broadcasted_iota(jnp.int32, x.shape, 1)
    g = -jnp.exp(a_ref[...]) * _softplus(x + dt_ref[...])
    o_ref[...] = jnp.where(lane % (2 * GDN_V_HEADS) >= GDN_V_HEADS, g, _sigmoid(x))


def _gdn_gates(gates, a_log, dt_bias, rows, tm):
    zeros = jnp.zeros_like(a_log)
    a_row = jnp.stack([zeros, a_log], 1).reshape(1, LANES)
    dt_row = jnp.stack([zeros, dt_bias], 1).reshape(1, LANES)
    return pl.pallas_call(
        _gdn_gate_kernel,
        out_shape=jax.ShapeDtypeStruct((rows, LANES), F32),
        grid=(rows // tm,),
        in_specs=[pl.BlockSpec((tm, LANES), lambda i: (i, 0)),
                  pl.BlockSpec((1, LANES), lambda i: (0, 0)),
                  pl.BlockSpec((1, LANES), lambda i: (0, 0))],
        out_specs=pl.BlockSpec((tm, LANES), lambda i: (i, 0)),
        compiler_params=_cparams(("parallel",)),
        name="gdn_gates",
    )(gates, a_row, dt_row)


def _chunk_masks(d):
    row = lax.broadcasted_iota(jnp.int32, (CHUNK, 2 * CHUNK), 0)
    col = lax.broadcasted_iota(jnp.int32, (CHUNK, 2 * CHUNK), 1)
    rel = (row - col % CHUNK) * (1 - 2 * d)
    return rel >= 0, rel > 0, col < CHUNK, row == col - CHUNK


def _cumsum_tables(x, d):
    r2 = lax.broadcasted_iota(jnp.int32, (2 * CHUNK, CHUNK), 0) % CHUNK
    c2 = lax.broadcasted_iota(jnp.int32, (2 * CHUNK, CHUNK), 1)
    incl2 = jnp.where((r2 - c2) * (1 - 2 * d) >= 0, 1.0, 0.0)
    cs2 = jnp.dot(incl2, x, preferred_element_type=F32, precision=lax.Precision.HIGHEST)
    return cs2[:CHUNK], cs2.T


def _split_bf16(x):
    hi = x.astype(BF16)
    return hi, (x - hi.astype(F32)).astype(BF16)


def _block_diag_lhs(x, keep, top):
    return jnp.concatenate([jnp.where(keep & top, x, 0.0), jnp.where(keep & ~top, x, 0.0)], axis=1)


def _gdn_scan_kernel(qf_ref, kf_ref, vf_ref, bf_ref, gf_ref, qb_ref, kb_ref, vb_ref, bb_ref, gb_ref,
                     of_ref, ob_ref, s_ref):
    @pl.when(pl.program_id(2) == 0)
    def _():
        s_ref[...] = jnp.zeros_like(s_ref)

    rep = GDN_V_HEADS // GDN_K_HEADS
    n_heads = GDN_GROUP * rep
    streams = ((qf_ref, kf_ref, vf_ref, bf_ref, gf_ref, of_ref), (qb_ref, kb_ref, vb_ref, bb_ref, gb_ref, ob_ref))
    row4 = lax.broadcasted_iota(jnp.int32, (GDN_CHAIN * CHUNK, 2 * CHUNK), 0)
    col4 = lax.broadcasted_iota(jnp.int32, (GDN_CHAIN * CHUNK, 2 * CHUNK), 1)
    p_blk = (row4 // CHUNK) % 2 == col4 // CHUNK
    top = row4 < 2 * CHUNK

    heads = []
    chains = []
    for d, (q_ref, k_ref, v_ref, beta_ref, g_ref, o_ref) in enumerate(streams):
        incl, strict, left, eye_right = _chunk_masks(d)
        eye_left = jnp.logical_and(incl, jnp.logical_and(~strict, left))
        g_all = g_ref[...]
        gc, gct = _cumsum_tables(g_all, d)
        g_tot = jnp.sum(g_all, axis=0, keepdims=True)
        beta_all = beta_ref[...]
        for kh in range(GDN_GROUP):
            ksl = slice(kh * GDN_DK, (kh + 1) * GDN_DK)
            kk_b = k_ref[:, ksl].astype(BF16)
            gram = _dot_nt(jnp.concatenate([kk_b, q_ref[:, ksl].astype(BF16)], axis=0),
                           jnp.concatenate([kk_b, kk_b], axis=0))
            for r in range(rep):
                h = kh * rep + r
                beta = beta_all[:, h:h + 1]
                gcol = gc[:, h:h + 1]
                decay = jnp.exp(jnp.where(incl, gcol - gct[h:h + 1, :], -jnp.inf))
                n_mat = -jnp.where(strict, beta * gram[:CHUNK] * decay, 0.0)
                if h % 2 == 0:
                    ps = jnp.where(left, n_mat, jnp.where(eye_right, 1.0, 0.0))
                else:
                    ps = jnp.where(left, jnp.where(eye_left, 1.0, 0.0), n_mat)
                heads.append(dict(d=d, h=h, ksl=ksl, k_ref=k_ref, q_ref=q_ref, v_ref=v_ref, o_ref=o_ref,
                                  beta=beta, gcol=gcol, tail=g_tot[:, h:h + 1], ps=ps,
                                  attn=gram[CHUNK:] * decay))
        for c in range(n_heads // GDN_CHAIN):
            chains.append([hd for hd in heads if hd["d"] == d][c * GDN_CHAIN:(c + 1) * GDN_CHAIN])
    stacks = [jnp.concatenate([hd["ps"] for hd in ch], axis=0) for ch in chains]

    for _ in range(int(math.log2(CHUNK))):
        for ci in range(len(chains)):
            ps = stacks[ci]
            lhs_hi, lhs_lo = _split_bf16(_block_diag_lhs(ps, p_blk, top))
            ps_hi, ps_lo = _split_bf16(ps)
            wide = _dot(lhs_hi, jnp.concatenate([ps_hi, ps_lo], axis=1))
            stacks[ci] = (wide[:, :2 * CHUNK] + wide[:, 2 * CHUNK:] + _dot(lhs_lo, ps_hi)
                          + jnp.where(p_blk, 0.0, ps))

    for ci, ch in enumerate(chains):
        t_lhs = _block_diag_lhs(stacks[ci], ~p_blk, top).astype(BF16)
        rhs = []
        for j in (1, 0, 3, 2):
            hd = ch[j]
            kb = hd["k_ref"][:, hd["ksl"]] * hd["beta"]
            vsl = slice(hd["h"] * GDN_DV, (hd["h"] + 1) * GDN_DV)
            rhs.append(jnp.concatenate([(hd["v_ref"][:, vsl] * hd["beta"]).astype(BF16),
                                        (kb * jnp.exp(hd["gcol"])).astype(BF16)], axis=1))
        w = _dot(t_lhs, jnp.concatenate(rhs, axis=0))
        for j, hd in enumerate(ch):
            hd["value"] = w[j * CHUNK:(j + 1) * CHUNK, :GDN_DV]
            hd["k_cum"] = w[j * CHUNK:(j + 1) * CHUNK, GDN_DV:]

    for hd in heads:
        state = s_ref[hd["d"] * n_heads + hd["h"]]
        q_dec = hd["q_ref"][:, hd["ksl"]] * jnp.exp(hd["gcol"])
        ks_qs = _dot(jnp.concatenate([hd["k_cum"].astype(BF16), q_dec.astype(BF16)], axis=0), state.astype(BF16))
        hd["state"] = state
        hd["v_new"] = (hd["value"] - ks_qs[:CHUNK]).astype(BF16)
        hd["qs"] = ks_qs[CHUNK:]
    for ch in chains:
        a_lhs = _block_diag_lhs(jnp.concatenate([hd["attn"] for hd in ch], axis=0), p_blk, top).astype(BF16)
        o_attn = _dot(a_lhs, jnp.concatenate([hd["v_new"] for hd in ch], axis=0))
        for j, hd in enumerate(ch):
            hd["o_ref"][:, hd["h"] * GDN_DV:(hd["h"] + 1) * GDN_DV] = hd["qs"] + o_attn[j * CHUNK:(j + 1) * CHUNK]
    for hd in heads:
        k_tail = hd["k_ref"][:, hd["ksl"]] * jnp.exp(hd["tail"] - hd["gcol"])
        s_ref[hd["d"] * n_heads + hd["h"]] = (hd["state"] * jnp.exp(hd["tail"])
                                              + _dot_tn(k_tail.astype(BF16), hd["v_new"]))


def _scan_row_block(d, b, s, seq, batch, n_ctx):
    nc = n_ctx // CHUNK
    nl = seq // CHUNK
    ctx_pos = jnp.where(d == 0, s, nc - 1 - s)
    lat_pos = jnp.where(d == 0, s - nc, nl - 1 - (s - nc))
    return jnp.where(s < nc, (batch * seq + b * n_ctx) // CHUNK + ctx_pos, b * nl + lat_pos)


def _group_gates(x, n_heads, group, rows):
    ng = n_heads // group
    x = x.reshape(rows, 2, 2, ng, group).transpose(2, 1, 3, 0, 4)
    x = jnp.pad(x, ((0, 0), (0, 0), (0, 0), (0, 0), (0, LANES - group)))
    return x[0], x[1]


def _gdn_scan(act, beta, g, dims, n_ctx, rows):
    d_model, seq, batch = dims
    rep = GDN_V_HEADS // GDN_K_HEADS
    ng = GDN_K_HEADS // GDN_GROUP
    gw = GDN_GROUP * GDN_DK
    vw = GDN_GROUP * rep * GDN_DV
    k_blk0 = GDN_K_HEADS * GDN_DK // gw
    v_blk0 = 2 * GDN_K_HEADS * GDN_DK // vw
    steps = (n_ctx + seq) // CHUNK
    rb = functools.partial(_scan_row_block, seq=seq, batch=batch, n_ctx=n_ctx)
    in_specs = []
    for d in range(2):
        in_specs += [pl.BlockSpec((CHUNK, gw), lambda b, hg, s, d=d: (rb(d, b, s), hg)),
                     pl.BlockSpec((CHUNK, gw), lambda b, hg, s, d=d: (rb(d, b, s), k_blk0 + hg)),
                     pl.BlockSpec((CHUNK, vw), lambda b, hg, s, d=d: (rb(d, b, s), v_blk0 + hg)),
                     pl.BlockSpec((None, None, CHUNK, LANES), lambda b, hg, s, d=d: (d, hg, rb(d, b, s), 0)),
                     pl.BlockSpec((None, None, CHUNK, LANES), lambda b, hg, s, d=d: (d, hg, rb(d, b, s), 0))]
    out = jax.ShapeDtypeStruct((rows, GDN_V_HEADS * GDN_DV), F32)
    return pl.pallas_call(
        _gdn_scan_kernel,
        out_shape=(out, out),
        grid=(batch, ng, steps),
        in_specs=in_specs,
        out_specs=tuple(pl.BlockSpec((CHUNK, vw), lambda b, hg, s, d=d: (rb(d, b, s), hg)) for d in range(2)),
        scratch_shapes=[pltpu.VMEM((2 * GDN_GROUP * rep, GDN_DK, GDN_DV), F32)],
        compiler_params=_cparams(("parallel", "parallel", "arbitrary")),
        name="gdn_scan",
    )(act, act, act, beta, g, act, act, act, beta, g)


def _gdn_out_kernel(of_ref, ob_ref, z_ref, res_ref, gn_ref, gate_ref, w_ref, out_ref, acc_ref):
    k = pl.program_id(1)

    @pl.when(k == 0)
    def _():
        acc_ref[...] = jnp.zeros_like(acc_ref)

    o = of_ref[...] + ob_ref[...]
    z = z_ref[...]
    gn = gn_ref[...]
    parts = []
    for h in range(o.shape[1] // GDN_DV):
        seg = o[:, h * GDN_DV:(h + 1) * GDN_DV]
        zz = z[:, h * GDN_DV:(h + 1) * GDN_DV]
        seg = seg * lax.rsqrt(jnp.mean(seg * seg, -1, keepdims=True) + EPS) * gn
        parts.append((seg * (zz * _sigmoid(zz))).astype(BF16))
    acc_ref[...] += _dot(jnp.concatenate(parts, axis=1), w_ref[...])

    @pl.when(k == pl.num_programs(1) - 1)
    def _():
        out_ref[...] = res_ref[...] + gate_ref[0] * acc_ref[...]


def _gdn_out(o, qkvz, res, out_norm, mods, w_o, dims, rows, tm, tkk):
    d = dims[0]
    width = o[0].shape[1]
    z_blk0 = (qkvz.shape[1] - width) // tkk
    return pl.pallas_call(
        _gdn_out_kernel,
        out_shape=jax.ShapeDtypeStruct((rows, d), F32),
        grid=(rows // tm, width // tkk),
        in_specs=[pl.BlockSpec((tm, tkk), lambda i, k: (i, k)),
                  pl.BlockSpec((tm, tkk), lambda i, k: (i, k)),
                  pl.BlockSpec((tm, tkk), lambda i, k: (i, z_blk0 + k)),
                  pl.BlockSpec((tm, d), lambda i, k: (i, 0)),
                  pl.BlockSpec((1, GDN_DV), lambda i, k: (0, 0)),
                  _mod_spec(dims, tm, 2, 2),
                  pl.BlockSpec((tkk, d), lambda i, k: (k, 0))],
        out_specs=pl.BlockSpec((tm, d), lambda i, k: (i, 0)),
        scratch_shapes=[pltpu.VMEM((tm, d), F32)],
        compiler_params=_cparams(("parallel", "arbitrary")),
        name="gdn_out",
    )(o[0], o[1], qkvz, res, out_norm.reshape(1, GDN_DV), mods, w_o)


def _gdn_layer(hs, mods, w, dims, n_ctx):
    d, seq, batch = dims
    rows = hs.shape[0]
    norm1, w_in, conv_w, a_log, dt_bias, out_norm, w_o = w
    main_w = 2 * GDN_K_HEADS * GDN_DK + 2 * GDN_V_HEADS * GDN_DV
    qkvz = _nm_matmul(hs, norm1, mods, (0, 1), w_in[:, :main_w].astype(BF16), dims, rows, 512, 1024, name="gdn_w_in")
    gates = _nm_matmul(hs, norm1, mods, (0, 1), w_in[:, main_w:].astype(BF16), dims, rows, 512, LANES,
                       name="gdn_w_gates")
    act = _gdn_conv(qkvz, conv_w, dims, n_ctx, rows, 256, 1024)
    bg = _gdn_gates(gates, a_log, dt_bias, rows, 512)
    beta, g = _group_gates(bg, GDN_V_HEADS, GDN_GROUP * GDN_V_HEADS // GDN_K_HEADS, rows)
    o = _gdn_scan(act, beta, g, dims, n_ctx, rows)
    return _gdn_out(o, qkvz, hs, out_norm, mods, w_o.astype(BF16), dims, rows, 256, 1024)


def _ml_gate_kernel(x_ref, b_ref, o_ref):
    x = x_ref[...] + b_ref[...]
    lane = lax.broadcasted_iota(jnp.int32, x.shape, 1)
    o_ref[...] = jnp.where(lane % (2 * ML_HEADS) >= ML_HEADS, _log_sigmoid(x), x)


def _ml_gates(gates, gate_b, rows, tm):
    b_row = jnp.pad(gate_b.reshape(1, -1), ((0, 0), (0, LANES - 4 * ML_HEADS)))
    return pl.pallas_call(
        _ml_gate_kernel,
        out_shape=jax.ShapeDtypeStruct((rows, LANES), F32),
        grid=(rows // tm,),
        in_specs=[pl.BlockSpec((tm, LANES), lambda i: (i, 0)),
                  pl.BlockSpec((1, LANES), lambda i: (0, 0))],
        out_specs=pl.BlockSpec((tm, LANES), lambda i: (i, 0)),
        compiler_params=_cparams(("parallel",)),
        name="mlstm_gates",
    )(gates, b_row)


def _ml_scan_kernel(q_ref, k_ref, v_ref, li_ref, lf_ref, o_ref, c_ref, n_ref, m_ref):
    d = pl.program_id(0)

    @pl.when(pl.program_id(3) == 0)
    def _():
        c_ref[...] = jnp.zeros_like(c_ref)
        n_ref[...] = jnp.zeros_like(n_ref)
        m_ref[...] = jnp.zeros_like(m_ref)

    row = lax.broadcasted_iota(jnp.int32, (CHUNK, CHUNK), 0)
    col = lax.broadcasted_iota(jnp.int32, (CHUNK, CHUNK), 1)
    incl = (row - col) * (1 - 2 * d) >= 0
    lf = lf_ref[...]
    li = li_ref[...]
    bc, bct = _cumsum_tables(lf, d)
    b_tot = jnp.sum(lf, axis=0, keepdims=True)
    lit = jnp.concatenate([li, li], axis=0).T
    for h in range(ML_GROUP):
        qh = q_ref[:, h * ML_DQK:(h + 1) * ML_DQK]
        kh = k_ref[:, h * ML_DQK:(h + 1) * ML_DQK] * (ML_DQK ** -0.5)
        vh = v_ref[:, h * ML_DV:(h + 1) * ML_DV].astype(BF16)
        bcol = bc[:, h:h + 1]
        brow = bct[h:h + 1, :CHUNK]
        icol = li[:, h:h + 1]
        irow = lit[h:h + 1, :CHUNK]
        b_last = b_tot[:, h:h + 1]
        m = m_ref[h][0:1, 0:1]
        d_mat = jnp.where(incl, bcol - brow + irow, -jnp.inf)
        d_max = jnp.max(d_mat, axis=-1, keepdims=True)
        qh_b = qh.astype(BF16)
        qk = _dot_nt(qh_b, kh.astype(BF16))
        w_end = b_last - bcol + icol
        w_end_max = jnp.max(w_end, axis=0, keepdims=True)
        inter = bcol + m
        mt = jnp.maximum(d_max, inter)
        p = jnp.exp(d_mat - mt) * qk
        a_in = jnp.exp(inter - mt)
        c_mat = c_ref[h]
        n_vec = n_ref[h]
        num = _dot(p.astype(BF16), vh) + a_in * _dot(qh_b, c_mat.astype(BF16))
        den = jnp.sum(p, axis=-1, keepdims=True) + a_in * jnp.sum(qh * n_vec, axis=-1, keepdims=True)
        o_ref[:, h * ML_DV:(h + 1) * ML_DV] = num / jnp.maximum(jnp.abs(den), jnp.exp(-mt))
        m_new = jnp.maximum(b_last + m, w_end_max)
        carry_decay = jnp.exp(b_last + m - m_new)
        kw = kh * jnp.exp(w_end - m_new)
        c_ref[h] = carry_decay * c_mat + _dot_tn(kw.astype(BF16), vh)
        n_ref[h] = carry_decay * n_vec + jnp.sum(kw, axis=0, keepdims=True)
        m_ref[h] = jnp.broadcast_to(m_new, m_ref.shape[1:])


def _ml_scan(qkvo, li, lf, dims, n_ctx, rows):
    d_model, seq, batch = dims
    ng = ML_HEADS // ML_GROUP
    qw = ML_GROUP * ML_DQK
    vw = ML_GROUP * ML_DV
    k_blk0 = ML_HEADS * ML_DQK // qw
    v_blk0 = 2 * ML_HEADS * ML_DQK // vw
    steps = (n_ctx + seq) // CHUNK
    rb = functools.partial(_scan_row_block, seq=seq, batch=batch, n_ctx=n_ctx)
    return pl.pallas_call(
        _ml_scan_kernel,
        out_shape=jax.ShapeDtypeStruct((2, rows, ML_HEADS * ML_DV), F32),
        grid=(2, batch, ng, steps),
        in_specs=[pl.BlockSpec((CHUNK, qw), lambda d, b, hg, s: (rb(d, b, s), hg)),
                  pl.BlockSpec((CHUNK, qw), lambda d, b, hg, s: (rb(d, b, s), k_blk0 + hg)),
                  pl.BlockSpec((CHUNK, vw), lambda d, b, hg, s: (rb(d, b, s), v_blk0 + hg)),
                  pl.BlockSpec((None, None, CHUNK, LANES), lambda d, b, hg, s: (d, hg, rb(d, b, s), 0)),
                  pl.BlockSpec((None, None, CHUNK, LANES), lambda d, b, hg, s: (d, hg, rb(d, b, s), 0))],
        out_specs=pl.BlockSpec((None, CHUNK, vw), lambda d, b, hg, s: (d, rb(d, b, s), hg)),
        scratch_shapes=[pltpu.VMEM((ML_GROUP, ML_DQK, ML_DV), F32),
                        pltpu.VMEM((ML_GROUP, 1, ML_DQK), F32),
                        pltpu.VMEM((ML_GROUP, SUBLANES, LANES), F32)],
        compiler_params=_cparams(("parallel", "parallel", "parallel", "arbitrary")),
        name="mlstm_scan",
    )(qkvo, qkvo, qkvo, li, lf)


def _ml_out_kernel(h_ref, og_ref, res_ref, gn_ref, gate_ref, w_ref, out_ref):
    hs = h_ref[0] + h_ref[1]
    og = og_ref[...]
    gn = gn_ref[...]
    parts = []
    for h in range(ML_HEADS):
        sl = slice(h * ML_DV, (h + 1) * ML_DV)
        seg = hs[:, sl]
        seg = seg * lax.rsqrt(jnp.mean(seg * seg, -1, keepdims=True) + EPS) * gn[:, sl]
        parts.append((seg * _sigmoid(og[:, sl])).astype(BF16))
    out_ref[...] = res_ref[...] + gate_ref[0] * _dot(jnp.concatenate(parts, axis=1), w_ref[...])


def _ml_out(hh, qkvo, res, out_norm, mods, w_o, dims, rows, tm):
    d = dims[0]
    width = hh.shape[2]
    og_blk = (qkvo.shape[1] - width) // width
    return pl.pallas_call(
        _ml_out_kernel,
        out_shape=jax.ShapeDtypeStruct((rows, d), F32),
        grid=(rows // tm,),
        in_specs=[pl.BlockSpec((2, tm, width), lambda i: (0, i, 0)),
                  pl.BlockSpec((tm, width), lambda i: (i, og_blk)),
                  pl.BlockSpec((tm, d), lambda i: (i, 0)),
                  pl.BlockSpec((1, width), lambda i: (0, 0)),
                  _mod_spec(dims, tm, 2, 1),
                  pl.BlockSpec((width, d), lambda i: (0, 0))],
        out_specs=pl.BlockSpec((tm, d), lambda i: (i, 0)),
        compiler_params=_cparams(("parallel",)),
        name="mlstm_out",
    )(hh, qkvo, res, out_norm.reshape(1, width), mods, w_o)


def _mlstm_layer(hs, mods, w, dims, n_ctx):
    d, seq, batch = dims
    rows = hs.shape[0]
    norm1, w_in, gate_b, out_norm, w_o = w
    main_w = 2 * ML_HEADS * ML_DQK + 2 * ML_HEADS * ML_DV
    qkvo = _nm_matmul(hs, norm1, mods, (0, 1), w_in[:, :main_w].astype(BF16), dims, rows, 512, 1024,
                      name="mlstm_w_in")
    w_g = jnp.pad(w_in[:, main_w:], ((0, 0), (0, LANES - 4 * ML_HEADS))).astype(BF16)
    gates = _nm_matmul(hs, norm1, mods, (0, 1), w_g, dims, rows, 512, LANES, name="mlstm_w_gates")
    lg = _ml_gates(gates, gate_b, rows, 512)
    li, lf = _group_gates(lg[:, :4 * ML_HEADS], ML_HEADS, ML_GROUP, rows)
    hh = _ml_scan(qkvo, li, lf, dims, n_ctx, rows)
    return _ml_out(hh, qkvo, hs, out_norm, mods, w_o.astype(BF16), dims, rows, 256)


def _final_norm_kernel(x_ref, g_ref, o_ref):
    x = x_ref[...]
    o_ref[...] = x * lax.rsqrt(jnp.mean(x * x, -1, keepdims=True) + EPS) * g_ref[...]


def _final_norm(x, g, rows, tm):
    d = x.shape[1]
    return pl.pallas_call(
        _final_norm_kernel,
        out_shape=jax.ShapeDtypeStruct((rows, d), F32),
        grid=(rows // tm,),
        in_specs=[pl.BlockSpec((tm, d), lambda i: (i, 0)),
                  pl.BlockSpec((1, d), lambda i: (0, 0))],
        out_specs=pl.BlockSpec((tm, d), lambda i: (i, 0)),
        compiler_params=_cparams(("parallel",)),
        name="final_norm",
    )(x, g.reshape(1, d))


def kernel(x, c, ctx, c_ctx, l0_ada_w, l0_ada_b, l0_norm1, l0_mla_w_in, l0_mla_q_norm, l0_mla_kv_norm, l0_mla_w_q_up, l0_mla_w_kv_up, l0_mla_w_o, l0_norm2, l0_mlp_w1, l0_mlp_w2, l1_ada_w, l1_ada_b, l1_norm1, l1_gdn_w_in, l1_gdn_conv_w, l1_gdn_a_log, l1_gdn_dt_bias, l1_gdn_out_norm, l1_gdn_w_o, l1_norm2, l1_mlp_w1, l1_mlp_w2, l2_ada_w, l2_ada_b, l2_norm1, l2_mlstm_w_in, l2_mlstm_gate_b, l2_mlstm_out_norm, l2_mlstm_w_o, l2_norm2, l2_mlp_w1, l2_mlp_w2, l3_ada_w, l3_ada_b, l3_norm1, l3_mla_w_in, l3_mla_q_norm, l3_mla_kv_norm, l3_mla_w_q_up, l3_mla_w_kv_up, l3_mla_w_o, l3_norm2, l3_mlp_w1, l3_mlp_w2, final_norm):
    batch, seq, d = x.shape
    n_ctx = ctx.shape[1]
    dims = (d, seq, batch)
    lat_rows = batch * seq
    rows = lat_rows + batch * n_ctx
    layers = (
        ("mla", (l0_ada_w, l0_ada_b, l0_norm2, l0_mlp_w1, l0_mlp_w2),
         (l0_norm1, l0_mla_w_in, l0_mla_q_norm, l0_mla_kv_norm, l0_mla_w_q_up, l0_mla_w_kv_up, l0_mla_w_o)),
        ("gdn", (l1_ada_w, l1_ada_b, l1_norm2, l1_mlp_w1, l1_mlp_w2),
         (l1_norm1, l1_gdn_w_in, l1_gdn_conv_w, l1_gdn_a_log, l1_gdn_dt_bias, l1_gdn_out_norm, l1_gdn_w_o)),
        ("mlstm", (l2_ada_w, l2_ada_b, l2_norm2, l2_mlp_w1, l2_mlp_w2),
         (l2_norm1, l2_mlstm_w_in, l2_mlstm_gate_b, l2_mlstm_out_norm, l2_mlstm_w_o)),
        ("mla", (l3_ada_w, l3_ada_b, l3_norm2, l3_mlp_w1, l3_mlp_w2),
         (l3_norm1, l3_mla_w_in, l3_mla_q_norm, l3_mla_kv_norm, l3_mla_w_q_up, l3_mla_w_kv_up, l3_mla_w_o)),
    )
    cond = jnp.concatenate([c, c_ctx[None, :], jnp.zeros((SUBLANES - batch - 1, d), F32)], 0)
    cos, sin = _rope_tables(seq)
    ctx_pad = ((0, batch * n_ctx), (0, 0))
    rope = (jnp.pad(jnp.tile(cos, (batch, 1)), ctx_pad, constant_values=1.0),
            jnp.pad(jnp.tile(sin, (batch, 1)), ctx_pad))
    hs = jnp.concatenate([x.reshape(lat_rows, d), ctx.reshape(batch * n_ctx, d)], 0)
    for li, (kind, (ada_w, ada_b, norm2, mlp_w1, mlp_w2), mixer_w) in enumerate(layers):
        ctx_out = li < len(layers) - 1
        mods = _ada(cond, ada_w, ada_b)[:batch + 1].reshape(batch + 1, 1, N_MOD * d)
        if kind == "mla":
            hs = _mla_layer(hs, mods, mixer_w, dims, n_ctx, ctx_out, rope)
        elif kind == "gdn":
            hs = _gdn_layer(hs, mods, mixer_w, dims, n_ctx)
        else:
            hs = _mlstm_layer(hs, mods, mixer_w, dims, n_ctx)
        hs = _mlp(hs, norm2, mods, mlp_w1.astype(BF16), mlp_w2.astype(BF16), dims, hs.shape[0], 512, 512)
    return _final_norm(hs, final_norm, lat_rows, 512).reshape(batch, seq, d)
```

```python
import functools
import math

import jax
import jax.numpy as jnp
from jax import lax
from jax.experimental import pallas as pl
from jax.experimental.pallas import tpu as pltpu

F32 = jnp.float32
BF16 = jnp.bfloat16

EPS = 1e-6
ROPE_THETA = 10000.0
GRID_W = 64
N_MOD = 6

MLA_HEADS = 16
MLA_Q_RANK = 768
MLA_KV_RANK = 512
MLA_NOPE = 128
MLA_ROPE = 64
MLA_V = 128
MLA_QK_PAD = 256
ATTN_UNROLL = 8

GDN_K_HEADS = 16
GDN_V_HEADS = 32
GDN_DK = 128
GDN_DV = 128
GDN_CONV = 5
GDN_GROUP = 4
GDN_CHAIN = 4

ML_HEADS = 8
ML_DQK = 128
ML_DV = 256
ML_GROUP = 8

CHUNK = 64
LANES = 128
SUBLANES = 8

VMEM_LIMIT = 52 * 1024 * 1024


def _cparams(sem):
    return pltpu.CompilerParams(dimension_semantics=sem, vmem_limit_bytes=VMEM_LIMIT)


def _dot(a, b):
    return jnp.dot(a, b, preferred_element_type=F32)


def _dot_nt(a, b):
    return lax.dot_general(a, b, (((1,), (1,)), ((), ())), preferred_element_type=F32)


def _dot_tn(a, b):
    return lax.dot_general(a, b, (((0,), (0,)), ((), ())), preferred_element_type=F32)


def _normmod(x, g, shift, scale):
    y = x * lax.rsqrt(jnp.mean(x * x, axis=-1, keepdims=True) + EPS) * g
    return y * (1.0 + scale) + shift


def _sigmoid(x):
    return 1.0 / (1.0 + jnp.exp(-x))


def _softplus(x):
    return jnp.maximum(x, 0.0) + jnp.log(1.0 + jnp.exp(-jnp.abs(x)))


def _log_sigmoid(x):
    return -_softplus(-x)


def _ada_kernel(c_ref, w_ref, b_ref, o_ref):
    c = c_ref[...]
    s = (c * _sigmoid(c)).astype(BF16)
    o_ref[...] = _dot(s, w_ref[...].astype(BF16)) + b_ref[...]


def _ada(cond, w, b):
    rows, d = cond.shape
    n = w.shape[1]
    tn = 1024
    return pl.pallas_call(
        _ada_kernel,
        out_shape=jax.ShapeDtypeStruct((rows, n), F32),
        grid=(n // tn,),
        in_specs=[pl.BlockSpec((rows, d), lambda j: (0, 0)),
                  pl.BlockSpec((d, tn), lambda j: (0, j)),
                  pl.BlockSpec((1, tn), lambda j: (0, j))],
        out_specs=pl.BlockSpec((rows, tn), lambda j: (0, j)),
        compiler_params=_cparams(("arbitrary",)),
        name="ada_mod",
    )(cond, w, b.reshape(1, n))


def _nm_matmul_kernel(x_ref, g_ref, sh_ref, sc_ref, w_ref, o_ref, u_ref):
    @pl.when(pl.program_id(1) == 0)
    def _():
        u_ref[...] = _normmod(x_ref[...], g_ref[...], sh_ref[0], sc_ref[0]).astype(BF16)

    o_ref[...] = _dot(u_ref[...], w_ref[...]).astype(o_ref.dtype)


def _mod_spec(dims, tm, kind, ngrid):
    d, seq, batch = dims
    if ngrid == 1:
        return pl.BlockSpec((1, 1, d), lambda i: (jnp.minimum(i * tm // seq, batch), 0, kind))
    return pl.BlockSpec((1, 1, d), lambda i, j: (jnp.minimum(i * tm // seq, batch), 0, kind))


def _nm_matmul(x, g, mods, kinds, w, dims, rows, tm, tn, out_dtype=F32, name="nm_matmul"):
    d = dims[0]
    n = w.shape[1]
    return pl.pallas_call(
        _nm_matmul_kernel,
        out_shape=jax.ShapeDtypeStruct((rows, n), out_dtype),
        grid=(rows // tm, n // tn),
        in_specs=[pl.BlockSpec((tm, d), lambda i, j: (i, 0)),
                  pl.BlockSpec((1, d), lambda i, j: (0, 0)),
                  _mod_spec(dims, tm, kinds[0], 2),
                  _mod_spec(dims, tm, kinds[1], 2),
                  pl.BlockSpec((d, tn), lambda i, j: (0, j))],
        out_specs=pl.BlockSpec((tm, tn), lambda i, j: (i, j)),
        scratch_shapes=[pltpu.VMEM((tm, d), BF16)],
        compiler_params=_cparams(("parallel", "arbitrary")),
        name=name,
    )(x, g.reshape(1, d), mods, mods, w)


def _mlp_kernel(x_ref, g_ref, sh_ref, sc_ref, gate_ref, w1_ref, w2_ref, o_ref, u_ref, acc_ref):
    k = pl.program_id(1)

    @pl.when(k == 0)
    def _():
        u_ref[...] = _normmod(x_ref[...], g_ref[...], sh_ref[0], sc_ref[0]).astype(BF16)
        acc_ref[...] = jnp.zeros_like(acc_ref)

    h = jnp.maximum(_dot(u_ref[...], w1_ref[...]), 0.0)
    acc_ref[...] += _dot((h * h).astype(BF16), w2_ref[...])

    @pl.when(k == pl.num_programs(1) - 1)
    def _():
        o_ref[...] = x_ref[...] + gate_ref[0] * acc_ref[...]


def _mlp(x, g, mods, w1, w2, dims, rows, tm, tk):
    d = dims[0]
    hidden = w1.shape[1]
    return pl.pallas_call(
        _mlp_kernel,
        out_shape=jax.ShapeDtypeStruct((rows, d), F32),
        grid=(rows // tm, hidden // tk),
        in_specs=[pl.BlockSpec((tm, d), lambda i, k: (i, 0)),
                  pl.BlockSpec((1, d), lambda i, k: (0, 0)),
                  _mod_spec(dims, tm, 3, 2),
                  _mod_spec(dims, tm, 4, 2),
                  _mod_spec(dims, tm, 5, 2),
                  pl.BlockSpec((d, tk), lambda i, k: (0, k)),
                  pl.BlockSpec((tk, d), lambda i, k: (k, 0))],
        out_specs=pl.BlockSpec((tm, d), lambda i, k: (i, 0)),
        scratch_shapes=[pltpu.VMEM((tm, d), BF16), pltpu.VMEM((tm, d), F32)],
        compiler_params=_cparams(("parallel", "arbitrary")),
        name="mlp",
    )(x, g.reshape(1, d), mods, mods, mods, w1, w2)


def _proj_res_kernel(*refs, n_lat_blocks):
    if n_lat_blocks is None:
        a_ref, res_ref, gate_ref, w_ref, o_ref = refs
        o_ref[...] = res_ref[...] + gate_ref[0] * _dot(a_ref[...], w_ref[...])
        return
    a_ref, ac_ref, res_ref, gate_ref, w_ref, o_ref = refs
    is_lat = pl.program_id(0) < n_lat_blocks

    @pl.when(is_lat)
    def _():
        o_ref[...] = res_ref[...] + gate_ref[0] * _dot(a_ref[...], w_ref[...])

    @pl.when(jnp.logical_not(is_lat))
    def _():
        o_ref[...] = res_ref[...] + gate_ref[0] * _dot(ac_ref[...], w_ref[...])


def _proj_res(a, a_ctx, res, mods, w, dims, rows, tm):
    d = dims[0]
    kdim = a.shape[1]
    nl = a.shape[0] // tm
    a_specs = [pl.BlockSpec((tm, kdim), lambda i: (jnp.minimum(i, nl - 1), 0))]
    a_args = [a]
    if a_ctx is not None:
        a_specs.append(pl.BlockSpec((tm, kdim), lambda i: (jnp.maximum(i - nl, 0), 0)))
        a_args.append(a_ctx)
    return pl.pallas_call(
        functools.partial(_proj_res_kernel, n_lat_blocks=None if a_ctx is None else nl),
        out_shape=jax.ShapeDtypeStruct((rows, d), F32),
        grid=(rows // tm,),
        in_specs=a_specs + [pl.BlockSpec((tm, d), lambda i: (i, 0)),
                            _mod_spec(dims, tm, 2, 1),
                            pl.BlockSpec((kdim, d), lambda i: (0, 0))],
        out_specs=pl.BlockSpec((tm, d), lambda i: (i, 0)),
        compiler_params=_cparams(("parallel",)),
        name="proj_res",
    )(*a_args, res, mods, w)


def _rope(x, cos, sin):
    lane = lax.broadcasted_iota(jnp.int32, x.shape, 1)
    partner = jnp.where(lane % 32 < 16, pltpu.roll(x, LANES - 16, 1), pltpu.roll(x, 16, 1))
    return x * cos + partner * sin


def _mla_up_kernel(c_ref, qn_ref, kvn_ref, wqn_ref, wqr_ref, wkn_ref, wv_ref, cos_ref, sin_ref,
                   q_ref, k_ref, v_ref):
    c = c_ref[...]
    cq = c[:, :MLA_Q_RANK]
    ckv = c[:, MLA_Q_RANK:MLA_Q_RANK + MLA_KV_RANK]
    kr = c[:, MLA_Q_RANK + MLA_KV_RANK:]
    cq = (cq * lax.rsqrt(jnp.mean(cq * cq, -1, keepdims=True) + EPS) * qn_ref[...]).astype(BF16)
    ckv = (ckv * lax.rsqrt(jnp.mean(ckv * ckv, -1, keepdims=True) + EPS) * kvn_ref[...]).astype(BF16)
    cos = cos_ref[...]
    sin = sin_ref[...]
    scale = (MLA_NOPE + MLA_ROPE) ** -0.5 * math.log2(math.e)
    qn = _dot(cq, wqn_ref[...]) * scale
    qr = _dot(cq, wqr_ref[...]) * scale
    kn = _dot(ckv, wkn_ref[...])
    v_ref[...] = _dot(ckv, wv_ref[...]).astype(v_ref.dtype)
    kr = _rope(kr, cos, sin).astype(k_ref.dtype)
    for h in range(MLA_HEADS):
        lo = h * MLA_QK_PAD
        sl = slice(h * LANES, (h + 1) * LANES)
        q_ref[:, lo:lo + LANES] = qn[:, sl].astype(q_ref.dtype)
        q_ref[:, lo + LANES:lo + 2 * LANES] = _rope(qr[:, sl], cos, sin).astype(q_ref.dtype)
        k_ref[:, lo:lo + LANES] = kn[:, sl].astype(k_ref.dtype)
        k_ref[:, lo + LANES:lo + 2 * LANES] = kr


def _mla_up(c, q_norm, kv_norm, wqn, wqr, wkn, wv, cos, sin, rows, tm):
    cw = c.shape[1]
    hq = MLA_HEADS * MLA_QK_PAD
    hv = MLA_HEADS * MLA_V
    full = lambda a: pl.BlockSpec(a.shape, lambda i: (0, 0))
    return pl.pallas_call(
        _mla_up_kernel,
        out_shape=(jax.ShapeDtypeStruct((rows, hq), BF16),
                   jax.ShapeDtypeStruct((rows, hq), BF16),
                   jax.ShapeDtypeStruct((rows, hv), BF16)),
        grid=(rows // tm,),
        in_specs=[pl.BlockSpec((tm, cw), lambda i: (i, 0)),
                  full(q_norm), full(kv_norm), full(wqn), full(wqr), full(wkn), full(wv),
                  pl.BlockSpec((tm, LANES), lambda i: (i, 0)),
                  pl.BlockSpec((tm, LANES), lambda i: (i, 0))],
        out_specs=(pl.BlockSpec((tm, hq), lambda i: (i, 0)),
                   pl.BlockSpec((tm, hq), lambda i: (i, 0)),
                   pl.BlockSpec((tm, hv), lambda i: (i, 0))),
        compiler_params=_cparams(("parallel",)),
        name="mla_up",
    )(c, q_norm, kv_norm, wqn, wqr, wkn, wv, cos, sin)


def _attn_kernel(*refs, n_lat, tk):
    if n_lat:
        q_ref, kl_ref, vl_ref, kc_ref, vc_ref, o_ref = refs
    else:
        q_ref, kc_ref, vc_ref, o_ref = refs
    q = q_ref[...]
    tq = q.shape[0]

    def step(k, v, carry):
        m, acc = carry
        s = _dot_nt(q, k)
        m_new = jnp.maximum(m, jnp.max(s, axis=-1, keepdims=True))
        p = jnp.exp2(s - m_new).astype(BF16)
        v_ext = jnp.concatenate([v, jnp.ones_like(v)], axis=1)
        return m_new, jnp.exp2(m - m_new) * acc + _dot(p, v_ext)

    carry = (jnp.full((tq, 1), -jnp.inf, F32), jnp.zeros((tq, 2 * MLA_V), F32))
    if n_lat:
        def body(i, carry):
            off = pl.multiple_of(i * tk, tk)
            return step(kl_ref[pl.ds(off, tk), :], vl_ref[pl.ds(off, tk), :], carry)

        carry = lax.fori_loop(0, n_lat // tk, body, carry, unroll=math.gcd(ATTN_UNROLL, n_lat // tk))
    m, acc = step(kc_ref[...], vc_ref[...], carry)
    o_ref[...] = (acc[:, :MLA_V] / acc[:, MLA_V:]).astype(o_ref.dtype)


def _attention(q, k, v, dims, n_ctx, tq, tk, ctx_queries):
    d, seq, batch = dims
    ctx_blk0 = batch * seq // n_ctx
    kc_spec = pl.BlockSpec((n_ctx, MLA_QK_PAD), lambda b, h, i: (ctx_blk0 + b, h))
    vc_spec = pl.BlockSpec((n_ctx, MLA_V), lambda b, h, i: (ctx_blk0 + b, h))
    if not ctx_queries:
        nq = seq // tq
        return pl.pallas_call(
            functools.partial(_attn_kernel, n_lat=seq, tk=tk),
            out_shape=jax.ShapeDtypeStruct((batch * seq, MLA_HEADS * MLA_V), BF16),
            grid=(batch, MLA_HEADS, nq),
            in_specs=[pl.BlockSpec((tq, MLA_QK_PAD), lambda b, h, i: (b * nq + i, h)),
                      pl.BlockSpec((seq, MLA_QK_PAD), lambda b, h, i: (b, h)),
                      pl.BlockSpec((seq, MLA_V), lambda b, h, i: (b, h)),
                      kc_spec, vc_spec],
            out_specs=pl.BlockSpec((tq, MLA_V), lambda b, h, i: (b * nq + i, h)),
            compiler_params=_cparams(("parallel", "parallel", "arbitrary")),
            name="mla_attention",
        )(q, k, v, k, v)
    return pl.pallas_call(
        functools.partial(_attn_kernel, n_lat=0, tk=tk),
        out_shape=jax.ShapeDtypeStruct((batch * n_ctx, MLA_HEADS * MLA_V), BF16),
        grid=(batch, MLA_HEADS, 1),
        in_specs=[pl.BlockSpec((n_ctx, MLA_QK_PAD), lambda b, h, i: (ctx_blk0 + b, h)), kc_spec, vc_spec],
        out_specs=pl.BlockSpec((n_ctx, MLA_V), lambda b, h, i: (b, h)),
        compiler_params=_cparams(("parallel", "parallel", "arbitrary")),
        name="mla_attention_ctx",
    )(q, k, v)


def _rope_tables(seq):
    t = jnp.arange(seq)
    row = (t // GRID_W).astype(F32)
    col = (t % GRID_W).astype(F32)
    axis_dim = MLA_ROPE // 2
    freqs = jnp.power(ROPE_THETA, -jnp.arange(0, axis_dim, 2, dtype=F32) / axis_dim)
    ar = row[:, None] * freqs[None, :]
    ac = col[:, None] * freqs[None, :]
    cr, sr, cc, sc = jnp.cos(ar), jnp.sin(ar), jnp.cos(ac), jnp.sin(ac)
    pad1 = jnp.ones((seq, LANES - MLA_ROPE), F32)
    pad0 = jnp.zeros((seq, LANES - MLA_ROPE), F32)
    cos = jnp.concatenate([cr, cr, cc, cc, pad1], -1)
    sin = jnp.concatenate([-sr, sr, -sc, sc, pad0], -1)
    return cos, sin


def _mla_layer(hs, mods, w, dims, n_ctx, ctx_out, rope):
    d, seq, batch = dims
    rows = hs.shape[0]
    lat_rows = batch * seq
    norm1, w_in, q_norm, kv_norm, w_q_up, w_kv_up, w_o = w
    c_w = MLA_Q_RANK + MLA_KV_RANK
    w_in_p = jnp.pad(w_in, ((0, 0), (0, LANES - MLA_ROPE))).astype(BF16)
    c = _nm_matmul(hs, norm1, mods, (0, 1), w_in_p, dims, rows, 256, w_in_p.shape[1], name="mla_w_in")
    wq = w_q_up.reshape(MLA_Q_RANK, MLA_HEADS, MLA_NOPE + MLA_ROPE)
    wqn = wq[:, :, :MLA_NOPE].reshape(MLA_Q_RANK, -1).astype(BF16)
    wqr = jnp.pad(wq[:, :, MLA_NOPE:], ((0, 0), (0, 0), (0, LANES - MLA_ROPE))).reshape(MLA_Q_RANK, -1).astype(BF16)
    wkv = w_kv_up.reshape(MLA_KV_RANK, MLA_HEADS, MLA_NOPE + MLA_V)
    wkn = wkv[:, :, :MLA_NOPE].reshape(MLA_KV_RANK, -1).astype(BF16)
    wv = wkv[:, :, MLA_NOPE:].reshape(MLA_KV_RANK, -1).astype(BF16)
    cos, sin = rope
    q, k, v = _mla_up(c, q_norm.reshape(1, -1), kv_norm.reshape(1, -1), wqn, wqr, wkn, wv, cos, sin, rows, 256)
    tq = min(512, seq)
    tk = min(512, seq)
    o = _attention(q, k, v, dims, n_ctx, tq, tk, False)
    o_ctx = _attention(q, k, v, dims, n_ctx, tq, tk, True) if ctx_out else None
    return _proj_res(o, o_ctx, hs, mods, w_o.astype(BF16), dims, rows if ctx_out else lat_rows, 256)


def _seq_edges(i, tm, seq, batch, n_ctx):
    r0 = i * tm
    lat = r0 < batch * seq
    rel = jnp.where(lat, r0 % seq, (r0 - batch * seq) % n_ctx)
    length = jnp.where(lat, seq, n_ctx)
    return rel == 0, rel + tm == length


def _gdn_conv_kernel(prev_ref, cur_ref, next_ref, w_ref, o_ref, ext_ref, *, tm, seq, batch, n_ctx, n_norm):
    i = pl.program_id(0)
    j = pl.program_id(1)
    first, last = _seq_edges(i, tm, seq, batch, n_ctx)
    ext_ref[0:SUBLANES, :] = jnp.where(first, 0.0, prev_ref[...])
    ext_ref[SUBLANES:SUBLANES + tm, :] = cur_ref[...]
    ext_ref[SUBLANES + tm:, :] = jnp.where(last, 0.0, next_ref[...])
    pad = GDN_CONV // 2
    y = None
    for t in range(GDN_CONV):
        term = w_ref[t:t + 1, :] * ext_ref[pl.ds(SUBLANES - pad + t, tm), :]
        y = term if y is None else y + term
    y = y * _sigmoid(y)

    @pl.when(j >= n_norm)
    def _():
        o_ref[...] = y

    @pl.when(j < n_norm)
    def _():
        sc = jnp.where(j < n_norm // 2, GDN_DK ** -0.5, 1.0)
        for h in range(y.shape[1] // GDN_DK):
            seg = y[:, h * GDN_DK:(h + 1) * GDN_DK]
            o_ref[:, h * GDN_DK:(h + 1) * GDN_DK] = seg * (lax.rsqrt(jnp.sum(seg * seg, -1, keepdims=True) + EPS) * sc)


def _gdn_conv(qkvz, conv_w, dims, n_ctx, rows, tm, tc):
    d, seq, batch = dims
    chans = conv_w.shape[1]
    n_norm = 2 * GDN_K_HEADS * GDN_DK // tc
    rb = tm // SUBLANES
    last_blk = rows // SUBLANES - 1
    return pl.pallas_call(
        functools.partial(_gdn_conv_kernel, tm=tm, seq=seq, batch=batch, n_ctx=n_ctx, n_norm=n_norm),
        out_shape=jax.ShapeDtypeStruct((rows, chans), F32),
        grid=(rows // tm, chans // tc),
        in_specs=[pl.BlockSpec((SUBLANES, tc), lambda i, j: (jnp.maximum(i * rb - 1, 0), j)),
                  pl.BlockSpec((tm, tc), lambda i, j: (i, j)),
                  pl.BlockSpec((SUBLANES, tc), lambda i, j: (jnp.minimum((i + 1) * rb, last_blk), j)),
                  pl.BlockSpec((GDN_CONV, tc), lambda i, j: (0, j))],
        out_specs=pl.BlockSpec((tm, tc), lambda i, j: (i, j)),
        scratch_shapes=[pltpu.VMEM((tm + 2 * SUBLANES, tc), F32)],
        compiler_params=_cparams(("parallel", "arbitrary")),
        name="gdn_conv",
    )(qkvz, qkvz, qkvz, conv_w)


def _gdn_gate_kernel(x_ref, a_ref, dt_ref, o_ref):
    x = x_ref[...]
    lane = lax.broadcasted_iota(jnp.int32, x.shape, 1)
    g = -jnp.exp(a_ref[...]) * _softplus(x + dt_ref[...])
    o_ref[...] = jnp.where(lane % (2 * GDN_V_HEADS) >= GDN_V_HEADS, g, _sigmoid(x))


def _gdn_gates(gates, a_log, dt_bias, rows, tm):
    zeros = jnp.zeros_like(a_log)
    a_row = jnp.stack([zeros, a_log], 1).reshape(1, LANES)
    dt_row = jnp.stack([zeros, dt_bias], 1).reshape(1, LANES)
    return pl.pallas_call(
        _gdn_gate_kernel,
        out_shape=jax.ShapeDtypeStruct((rows, LANES), F32),
        grid=(rows // tm,),
        in_specs=[pl.BlockSpec((tm, LANES), lambda i: (i, 0)),
                  pl.BlockSpec((1, LANES), lambda i: (0, 0)),
                  pl.BlockSpec((1, LANES), lambda i: (0, 0))],
        out_specs=pl.BlockSpec((tm, LANES), lambda i: (i, 0)),
        compiler_params=_cparams(("parallel",)),
        name="gdn_gates",
    )(gates, a_row, dt_row)


def _chunk_masks(d):
    row = lax.broadcasted_iota(jnp.int32, (CHUNK, 2 * CHUNK), 0)
    col = lax.broadcasted_iota(jnp.int32, (CHUNK, 2 * CHUNK), 1)
    rel = (row - col % CHUNK) * (1 - 2 * d)
    return rel >= 0, rel > 0, col < CHUNK, row == col - CHUNK


def _cumsum_tables(x, d):
    r2 = lax.broadcasted_iota(jnp.int32, (2 * CHUNK, CHUNK), 0) % CHUNK
    c2 = lax.broadcasted_iota(jnp.int32, (2 * CHUNK, CHUNK), 1)
    incl2 = jnp.where((r2 - c2) * (1 - 2 * d) >= 0, 1.0, 0.0)
    cs2 = jnp.dot(incl2, x, preferred_element_type=F32, precision=lax.Precision.HIGHEST)
    return cs2[:CHUNK], cs2.T


def _split_bf16(x):
    hi = x.astype(BF16)
    return hi, (x - hi.astype(F32)).astype(BF16)


def _block_diag_lhs(x, keep, top):
    return jnp.concatenate([jnp.where(keep & top, x, 0.0), jnp.where(keep & ~top, x, 0.0)], axis=1)


def _gdn_scan_kernel(qf_ref, kf_ref, vf_ref, bf_ref, gf_ref, qb_ref, kb_ref, vb_ref, bb_ref, gb_ref,
                     of_ref, ob_ref, s_ref):
    @pl.when(pl.program_id(2) == 0)
    def _():
        s_ref[...] = jnp.zeros_like(s_ref)

    rep = GDN_V_HEADS // GDN_K_HEADS
    n_heads = GDN_GROUP * rep
    streams = ((qf_ref, kf_ref, vf_ref, bf_ref, gf_ref, of_ref), (qb_ref, kb_ref, vb_ref, bb_ref, gb_ref, ob_ref))
    row4 = lax.broadcasted_iota(jnp.int32, (GDN_CHAIN * CHUNK, 2 * CHUNK), 0)
    col4 = lax.broadcasted_iota(jnp.int32, (GDN_CHAIN * CHUNK, 2 * CHUNK), 1)
    p_blk = (row4 // CHUNK) % 2 == col4 // CHUNK
    top = row4 < 2 * CHUNK

    heads = []
    chains = []
    for d, (q_ref, k_ref, v_ref, beta_ref, g_ref, o_ref) in enumerate(streams):
        incl, strict, left, eye_right = _chunk_masks(d)
        eye_left = jnp.logical_and(incl, jnp.logical_and(~strict, left))
        g_all = g_ref[...]
        gc, gct = _cumsum_tables(g_all, d)
        g_tot = jnp.sum(g_all, axis=0, keepdims=True)
        beta_all = beta_ref[...]
        for kh in range(GDN_GROUP):
            ksl = slice(kh * GDN_DK, (kh + 1) * GDN_DK)
            kk_b = k_ref[:, ksl].astype(BF16)
            gram = _dot_nt(jnp.concatenate([kk_b, q_ref[:, ksl].astype(BF16)], axis=0),
                           jnp.concatenate([kk_b, kk_b], axis=0))
            for r in range(rep):
                h = kh * rep + r
                beta = beta_all[:, h:h + 1]
                gcol = gc[:, h:h + 1]
                decay = jnp.exp(jnp.where(incl, gcol - gct[h:h + 1, :], -jnp.inf))
                n_mat = -jnp.where(strict, beta * gram[:CHUNK] * decay, 0.0)
                if h % 2 == 0:
                    ps = jnp.where(left, n_mat, jnp.where(eye_right, 1.0, 0.0))
                else:
                    ps = jnp.where(left, jnp.where(eye_left, 1.0, 0.0), n_mat)
                heads.append(dict(d=d, h=h, ksl=ksl, k_ref=k_ref, q_ref=q_ref, v_ref=v_ref, o_ref=o_ref,
                                  beta=beta, gcol=gcol, tail=g_tot[:, h:h + 1], ps=ps,
                                  attn=gram[CHUNK:] * decay))
        for c in range(n_heads // GDN_CHAIN):
            chains.append([hd for hd in heads if hd["d"] == d][c * GDN_CHAIN:(c + 1) * GDN_CHAIN])
    stacks = [jnp.concatenate([hd["ps"] for hd in ch], axis=0) for ch in chains]

    for _ in range(int(math.log2(CHUNK))):
        for ci in range(len(chains)):
            ps = stacks[ci]
            lhs_hi, lhs_lo = _split_bf16(_block_diag_lhs(ps, p_blk, top))
            ps_hi, ps_lo = _split_bf16(ps)
            wide = _dot(lhs_hi, jnp.concatenate([ps_hi, ps_lo], axis=1))
            stacks[ci] = (wide[:, :2 * CHUNK] + wide[:, 2 * CHUNK:] + _dot(lhs_lo, ps_hi)
                          + jnp.where(p_blk, 0.0, ps))

    for ci, ch in enumerate(chains):
        t_lhs = _block_diag_lhs(stacks[ci], ~p_blk, top).astype(BF16)
        rhs = []
        for j in (1, 0, 3, 2):
            hd = ch[j]
            kb = hd["k_ref"][:, hd["ksl"]] * hd["beta"]
            vsl = slice(hd["h"] * GDN_DV, (hd["h"] + 1) * GDN_DV)
            rhs.append(jnp.concatenate([(hd["v_ref"][:, vsl] * hd["beta"]).astype(BF16),
                                        (kb * jnp.exp(hd["gcol"])).astype(BF16)], axis=1))
        w = _dot(t_lhs, jnp.concatenate(rhs, axis=0))
        for j, hd in enumerate(ch):
            hd["value"] = w[j * CHUNK:(j + 1) * CHUNK, :GDN_DV]
            hd["k_cum"] = w[j * CHUNK:(j + 1) * CHUNK, GDN_DV:]

    for hd in heads:
        state = s_ref[hd["d"] * n_heads + hd["h"]]
        q_dec = hd["q_ref"][:, hd["ksl"]] * jnp.exp(hd["gcol"])
        ks_qs = _dot(jnp.concatenate([hd["k_cum"].astype(BF16), q_dec.astype(BF16)], axis=0), state.astype(BF16))
        hd["state"] = state
        hd["v_new"] = (hd["value"] - ks_qs[:CHUNK]).astype(BF16)
        hd["qs"] = ks_qs[CHUNK:]
    for ch in chains:
        a_lhs = _block_diag_lhs(jnp.concatenate([hd["attn"] for hd in ch], axis=0), p_blk, top).astype(BF16)
        o_attn = _dot(a_lhs, jnp.concatenate([hd["v_new"] for hd in ch], axis=0))
        for j, hd in enumerate(ch):
            hd["o_ref"][:, hd["h"] * GDN_DV:(hd["h"] + 1) * GDN_DV] = hd["qs"] + o_attn[j * CHUNK:(j + 1) * CHUNK]
    for hd in heads:
        k_tail = hd["k_ref"][:, hd["ksl"]] * jnp.exp(hd["tail"] - hd["gcol"])
        s_ref[hd["d"] * n_heads + hd["h"]] = (hd["state"] * jnp.exp(hd["tail"])
                                              + _dot_tn(k_tail.astype(BF16), hd["v_new"]))


def _scan_row_block(d, b, s, seq, batch, n_ctx):
    nc = n_ctx // CHUNK
    nl = seq // CHUNK
    ctx_pos = jnp.where(d == 0, s, nc - 1 - s)
    lat_pos = jnp.where(d == 0, s - nc, nl - 1 - (s - nc))
    return jnp.where(s < nc, (batch * seq + b * n_ctx) // CHUNK + ctx_pos, b * nl + lat_pos)


def _group_gates(x, n_heads, group, rows):
    ng = n_heads // group
    x = x.reshape(rows, 2, 2, ng, group).transpose(2, 1, 3, 0, 4)
    x = jnp.pad(x, ((0, 0), (0, 0), (0, 0), (0, 0), (0, LANES - group)))
    return x[0], x[1]


def _gdn_scan(act, beta, g, dims, n_ctx, rows):
    d_model, seq, batch = dims
    rep = GDN_V_HEADS // GDN_K_HEADS
    ng = GDN_K_HEADS // GDN_GROUP
    gw = GDN_GROUP * GDN_DK
    vw = GDN_GROUP * rep * GDN_DV
    k_blk0 = GDN_K_HEADS * GDN_DK // gw
    v_blk0 = 2 * GDN_K_HEADS * GDN_DK // vw
    steps = (n_ctx + seq) // CHUNK
    rb = functools.partial(_scan_row_block, seq=seq, batch=batch, n_ctx=n_ctx)
    in_specs = []
    for d in range(2):
        in_specs += [pl.BlockSpec((CHUNK, gw), lambda b, hg, s, d=d: (rb(d, b, s), hg)),
                     pl.BlockSpec((CHUNK, gw), lambda b, hg, s, d=d: (rb(d, b, s), k_blk0 + hg)),
                     pl.BlockSpec((CHUNK, vw), lambda b, hg, s, d=d: (rb(d, b, s), v_blk0 + hg)),
                     pl.BlockSpec((None, None, CHUNK, LANES), lambda b, hg, s, d=d: (d, hg, rb(d, b, s), 0)),
                     pl.BlockSpec((None, None, CHUNK, LANES), lambda b, hg, s, d=d: (d, hg, rb(d, b, s), 0))]
    out = jax.ShapeDtypeStruct((rows, GDN_V_HEADS * GDN_DV), F32)
    return pl.pallas_call(
        _gdn_scan_kernel,
        out_shape=(out, out),
        grid=(batch, ng, steps),
        in_specs=in_specs,
        out_specs=tuple(pl.BlockSpec((CHUNK, vw), lambda b, hg, s, d=d: (rb(d, b, s), hg)) for d in range(2)),
        scratch_shapes=[pltpu.VMEM((2 * GDN_GROUP * rep, GDN_DK, GDN_DV), F32)],
        compiler_params=_cparams(("parallel", "parallel", "arbitrary")),
        name="gdn_scan",
    )(act, act, act, beta, g, act, act, act, beta, g)


def _gdn_out_kernel(of_ref, ob_ref, z_ref, res_ref, gn_ref, gate_ref, w_ref, out_ref, acc_ref):
    k = pl.program_id(1)

    @pl.when(k == 0)
    def _():
        acc_ref[...] = jnp.zeros_like(acc_ref)

    o = of_ref[...] + ob_ref[...]
    z = z_ref[...]
    gn = gn_ref[...]
    parts = []
    for h in range(o.shape[1] // GDN_DV):
        seg = o[:, h * GDN_DV:(h + 1) * GDN_DV]
        zz = z[:, h * GDN_DV:(h + 1) * GDN_DV]
        seg = seg * lax.rsqrt(jnp.mean(seg * seg, -1, keepdims=True) + EPS) * gn
        parts.append((seg * (zz * _sigmoid(zz))).astype(BF16))
    acc_ref[...] += _dot(jnp.concatenate(parts, axis=1), w_ref[...])

    @pl.when(k == pl.num_programs(1) - 1)
    def _():
        out_ref[...] = res_ref[...] + gate_ref[0] * acc_ref[...]


def _gdn_out(o, qkvz, res, out_norm, mods, w_o, dims, rows, tm, tkk):
    d = dims[0]
    width = o[0].shape[1]
    z_blk0 = (qkvz.shape[1] - width) // tkk
    return pl.pallas_call(
        _gdn_out_kernel,
        out_shape=jax.ShapeDtypeStruct((rows, d), F32),
        grid=(rows // tm, width // tkk),
        in_specs=[pl.BlockSpec((tm, tkk), lambda i, k: (i, k)),
                  pl.BlockSpec((tm, tkk), lambda i, k: (i, k)),
                  pl.BlockSpec((tm, tkk), lambda i, k: (i, z_blk0 + k)),
                  pl.BlockSpec((tm, d), lambda i, k: (i, 0)),
                  pl.BlockSpec((1, GDN_DV), lambda i, k: (0, 0)),
                  _mod_spec(dims, tm, 2, 2),
                  pl.BlockSpec((tkk, d), lambda i, k: (k, 0))],
        out_specs=pl.BlockSpec((tm, d), lambda i, k: (i, 0)),
        scratch_shapes=[pltpu.VMEM((tm, d), F32)],
        compiler_params=_cparams(("parallel", "arbitrary")),
        name="gdn_out",
    )(o[0], o[1], qkvz, res, out_norm.reshape(1, GDN_DV), mods, w_o)


def _gdn_layer(hs, mods, w, dims, n_ctx):
    d, seq, batch = dims
    rows = hs.shape[0]
    norm1, w_in, conv_w, a_log, dt_bias, out_norm, w_o = w
    main_w = 2 * GDN_K_HEADS * GDN_DK + 2 * GDN_V_HEADS * GDN_DV
    qkvz = _nm_matmul(hs, norm1, mods, (0, 1), w_in[:, :main_w].astype(BF16), dims, rows, 512, 1024, name="gdn_w_in")
    gates = _nm_matmul(hs, norm1, mods, (0, 1), w_in[:, main_w:].astype(BF16), dims, rows, 512, LANES,
                       name="gdn_w_gates")
    act = _gdn_conv(qkvz, conv_w, dims, n_ctx, rows, 256, 1024)
    bg = _gdn_gates(gates, a_log, dt_bias, rows, 512)
    beta, g = _group_gates(bg, GDN_V_HEADS, GDN_GROUP * GDN_V_HEADS // GDN_K_HEADS, rows)
    o = _gdn_scan(act, beta, g, dims, n_ctx, rows)
    return _gdn_out(o, qkvz, hs, out_norm, mods, w_o.astype(BF16), dims, rows, 256, 1024)


def _ml_gate_kernel(x_ref, b_ref, o_ref):
    x = x_ref[...] + b_ref[...]
    lane = lax.broadcasted_iota(jnp.int32, x.shape, 1)
    o_ref[...] = jnp.where(lane % (2 * ML_HEADS) >= ML_HEADS, _log_sigmoid(x), x)


def _ml_gates(gates, gate_b, rows, tm):
    b_row = jnp.pad(gate_b.reshape(1, -1), ((0, 0), (0, LANES - 4 * ML_HEADS)))
    return pl.pallas_call(
        _ml_gate_kernel,
        out_shape=jax.ShapeDtypeStruct((rows, LANES), F32),
        grid=(rows // tm,),
        in_specs=[pl.BlockSpec((tm, LANES), lambda i: (i, 0)),
                  pl.BlockSpec((1, LANES), lambda i: (0, 0))],
        out_specs=pl.BlockSpec((tm, LANES), lambda i: (i, 0)),
        compiler_params=_cparams(("parallel",)),
        name="mlstm_gates",
    )(gates, b_row)


def _ml_scan_kernel(qf_ref, kf_ref, vf_ref, if_ref, ff_ref, qb_ref, kb_ref, vb_ref, ib_ref, fb_ref,
                    of_ref, ob_ref, c_ref, n_ref, m_ref):
    @pl.when(pl.program_id(2) == 0)
    def _():
        c_ref[...] = jnp.zeros_like(c_ref)
        n_ref[...] = jnp.zeros_like(n_ref)
        m_ref[...] = jnp.zeros_like(m_ref)

    streams = ((qf_ref, kf_ref, vf_ref, if_ref, ff_ref, of_ref), (qb_ref, kb_ref, vb_ref, ib_ref, fb_ref, ob_ref))
    row = lax.broadcasted_iota(jnp.int32, (CHUNK, CHUNK), 0)
    col = lax.broadcasted_iota(jnp.int32, (CHUNK, CHUNK), 1)

    heads = []
    for d, (q_ref, k_ref, v_ref, li_ref, lf_ref, o_ref) in enumerate(streams):
        incl = (row - col) * (1 - 2 * d) >= 0
        lf = lf_ref[...]
        li = li_ref[...]
        bc, bct = _cumsum_tables(lf, d)
        b_tot = jnp.sum(lf, axis=0, keepdims=True)
        lit = jnp.concatenate([li, li], axis=0).T
        for h in range(ML_GROUP):
            qh = q_ref[:, h * ML_DQK:(h + 1) * ML_DQK]
            kh = k_ref[:, h * ML_DQK:(h + 1) * ML_DQK] * (ML_DQK ** -0.5)
            bcol = bc[:, h:h + 1]
            b_last = b_tot[:, h:h + 1]
            d_mat = jnp.where(incl, bcol - bct[h:h + 1, :CHUNK] + lit[h:h + 1, :CHUNK], -jnp.inf)
            w_end = b_last - bcol + li[:, h:h + 1]
            heads.append(dict(idx=d * ML_GROUP + h, h=h, q_ref=q_ref, v_ref=v_ref, o_ref=o_ref, kh=kh,
                              bcol=bcol, b_last=b_last, d_mat=d_mat,
                              d_max=jnp.max(d_mat, axis=-1, keepdims=True),
                              qk=_dot_nt(qh.astype(BF16), kh.astype(BF16)),
                              w_end=w_end, w_end_max=jnp.max(w_end, axis=0, keepdims=True)))

    for hd in heads:
        hd["m"] = m_ref[hd["idx"]][0:1, 0:1]
        hd["c"] = c_ref[hd["idx"]]
        hd["nv"] = n_ref[hd["idx"]]
        hd["v"] = hd["v_ref"][:, hd["h"] * ML_DV:(hd["h"] + 1) * ML_DV].astype(BF16)
        inter = hd["bcol"] + hd["m"]
        hd["mt"] = jnp.maximum(hd["d_max"], inter)
        hd["p"] = jnp.exp(hd["d_mat"] - hd["mt"]) * hd["qk"]
        hd["a_in"] = jnp.exp(inter - hd["mt"])
    for hd in heads:
        qh = hd["q_ref"][:, hd["h"] * ML_DQK:(hd["h"] + 1) * ML_DQK]
        num = _dot(hd["p"].astype(BF16), hd["v"]) + hd["a_in"] * _dot(qh.astype(BF16), hd["c"].astype(BF16))
        den = (jnp.sum(hd["p"], axis=-1, keepdims=True)
               + hd["a_in"] * jnp.sum(qh * hd["nv"], axis=-1, keepdims=True))
        hd["o_ref"][:, hd["h"] * ML_DV:(hd["h"] + 1) * ML_DV] = num / jnp.maximum(jnp.abs(den), jnp.exp(-hd["mt"]))
    for hd in heads:
        m_new = jnp.maximum(hd["b_last"] + hd["m"], hd["w_end_max"])
        carry_decay = jnp.exp(hd["b_last"] + hd["m"] - m_new)
        kw = hd["kh"] * jnp.exp(hd["w_end"] - m_new)
        c_ref[hd["idx"]] = carry_decay * hd["c"] + _dot_tn(kw.astype(BF16), hd["v"])
        n_ref[hd["idx"]] = carry_decay * hd["nv"] + jnp.sum(kw, axis=0, keepdims=True)
        m_ref[hd["idx"]] = jnp.broadcast_to(m_new, m_ref.shape[1:])


def _ml_scan(qkvo, li, lf, dims, n_ctx, rows):
    d_model, seq, batch = dims
    ng = ML_HEADS // ML_GROUP
    qw = ML_GROUP * ML_DQK
    vw = ML_GROUP * ML_DV
    k_blk0 = ML_HEADS * ML_DQK // qw
    v_blk0 = 2 * ML_HEADS * ML_DQK // vw
    steps = (n_ctx + seq) // CHUNK
    rb = functools.partial(_scan_row_block, seq=seq, batch=batch, n_ctx=n_ctx)
    in_specs = []
    for d in range(2):
        in_specs += [pl.BlockSpec((CHUNK, qw), lambda b, hg, s, d=d: (rb(d, b, s), hg)),
                     pl.BlockSpec((CHUNK, qw), lambda b, hg, s, d=d: (rb(d, b, s), k_blk0 + hg)),
                     pl.BlockSpec((CHUNK, vw), lambda b, hg, s, d=d: (rb(d, b, s), v_blk0 + hg)),
                     pl.BlockSpec((None, None, CHUNK, LANES), lambda b, hg, s, d=d: (d, hg, rb(d, b, s), 0)),
                     pl.BlockSpec((None, None, CHUNK, LANES), lambda b, hg, s, d=d: (d, hg, rb(d, b, s), 0))]
    out = jax.ShapeDtypeStruct((rows, ML_HEADS * ML_DV), F32)
    return pl.pallas_call(
        _ml_scan_kernel,
        out_shape=(out, out),
        grid=(batch, ng, steps),
        in_specs=in_specs,
        out_specs=tuple(pl.BlockSpec((CHUNK, vw), lambda b, hg, s, d=d: (rb(d, b, s), hg)) for d in range(2)),
        scratch_shapes=[pltpu.VMEM((2 * ML_GROUP, ML_DQK, ML_DV), F32),
                        pltpu.VMEM((2 * ML_GROUP, 1, ML_DQK), F32),
                        pltpu.VMEM((2 * ML_GROUP, SUBLANES, LANES), F32)],
        compiler_params=_cparams(("parallel", "parallel", "arbitrary")),
        name="mlstm_scan",
    )(qkvo, qkvo, qkvo, li, lf, qkvo, qkvo, qkvo, li, lf)


def _ml_out_kernel(hf_ref, hb_ref, og_ref, res_ref, gn_ref, gate_ref, w_ref, out_ref):
    hs = hf_ref[...] + hb_ref[...]
    og = og_ref[...]
    gn = gn_ref[...]
    parts = []
    for h in range(ML_HEADS):
        sl = slice(h * ML_DV, (h + 1) * ML_DV)
        seg = hs[:, sl]
        seg = seg * lax.rsqrt(jnp.mean(seg * seg, -1, keepdims=True) + EPS) * gn[:, sl]
        parts.append((seg * _sigmoid(og[:, sl])).astype(BF16))
    out_ref[...] = res_ref[...] + gate_ref[0] * _dot(jnp.concatenate(parts, axis=1), w_ref[...])


def _ml_out(hh, qkvo, res, out_norm, mods, w_o, dims, rows, tm):
    d = dims[0]
    width = hh[0].shape[1]
    og_blk = (qkvo.shape[1] - width) // width
    return pl.pallas_call(
        _ml_out_kernel,
        out_shape=jax.ShapeDtypeStruct((rows, d), F32),
        grid=(rows // tm,),
        in_specs=[pl.BlockSpec((tm, width), lambda i: (i, 0)),
                  pl.BlockSpec((tm, width), lambda i: (i, 0)),
                  pl.BlockSpec((tm, width), lambda i: (i, og_blk)),
                  pl.BlockSpec((tm, d), lambda i: (i, 0)),
                  pl.BlockSpec((1, width), lambda i: (0, 0)),
                  _mod_spec(dims, tm, 2, 1),
                  pl.BlockSpec((width, d), lambda i: (0, 0))],
        out_specs=pl.BlockSpec((tm, d), lambda i: (i, 0)),
        compiler_params=_cparams(("parallel",)),
        name="mlstm_out",
    )(hh[0], hh[1], qkvo, res, out_norm.reshape(1, width), mods, w_o)


def _mlstm_layer(hs, mods, w, dims, n_ctx):
    d, seq, batch = dims
    rows = hs.shape[0]
    norm1, w_in, gate_b, out_norm, w_o = w
    main_w = 2 * ML_HEADS * ML_DQK + 2 * ML_HEADS * ML_DV
    qkvo = _nm_matmul(hs, norm1, mods, (0, 1), w_in[:, :main_w].astype(BF16), dims, rows, 512, 1024,
                      name="mlstm_w_in")
    w_g = jnp.pad(w_in[:, main_w:], ((0, 0), (0, LANES - 4 * ML_HEADS))).astype(BF16)
    gates = _nm_matmul(hs, norm1, mods, (0, 1), w_g, dims, rows, 512, LANES, name="mlstm_w_gates")
    lg = _ml_gates(gates, gate_b, rows, 512)
    li, lf = _group_gates(lg[:, :4 * ML_HEADS], ML_HEADS, ML_GROUP, rows)
    hh = _ml_scan(qkvo, li, lf, dims, n_ctx, rows)
    return _ml_out(hh, qkvo, hs, out_norm, mods, w_o.astype(BF16), dims, rows, 256)


def _final_norm_kernel(x_ref, g_ref, o_ref):
    x = x_ref[...]
    o_ref[...] = x * lax.rsqrt(jnp.mean(x * x, -1, keepdims=True) + EPS) * g_ref[...]


def _final_norm(x, g, rows, tm):
    d = x.shape[1]
    return pl.pallas_call(
        _final_norm_kernel,
        out_shape=jax.ShapeDtypeStruct((rows, d), F32),
        grid=(rows // tm,),
        in_specs=[pl.BlockSpec((tm, d), lambda i: (i, 0)),
                  pl.BlockSpec((1, d), lambda i: (0, 0))],
        out_specs=pl.BlockSpec((tm, d), lambda i: (i, 0)),
        compiler_params=_cparams(("parallel",)),
        name="final_norm",
    )(x, g.reshape(1, d))


def kernel(x, c, ctx, c_ctx, l0_ada_w, l0_ada_b, l0_norm1, l0_mla_w_in, l0_mla_q_norm, l0_mla_kv_norm, l0_mla_w_q_up, l0_mla_w_kv_up, l0_mla_w_o, l0_norm2, l0_mlp_w1, l0_mlp_w2, l1_ada_w, l1_ada_b, l1_norm1, l1_gdn_w_in, l1_gdn_conv_w, l1_gdn_a_log, l1_gdn_dt_bias, l1_gdn_out_norm, l1_gdn_w_o, l1_norm2, l1_mlp_w1, l1_mlp_w2, l2_ada_w, l2_ada_b, l2_norm1, l2_mlstm_w_in, l2_mlstm_gate_b, l2_mlstm_out_norm, l2_mlstm_w_o, l2_norm2, l2_mlp_w1, l2_mlp_w2, l3_ada_w, l3_ada_b, l3_norm1, l3_mla_w_in, l3_mla_q_norm, l3_mla_kv_norm, l3_mla_w_q_up, l3_mla_w_kv_up, l3_mla_w_o, l3_norm2, l3_mlp_w1, l3_mlp_w2, final_norm):
    batch, seq, d = x.shape
    n_ctx = ctx.shape[1]
    dims = (d, seq, batch)
    lat_rows = batch * seq
    rows = lat_rows + batch * n_ctx
    layers = (
        ("mla", (l0_ada_w, l0_ada_b, l0_norm2, l0_mlp_w1, l0_mlp_w2),
         (l0_norm1, l0_mla_w_in, l0_mla_q_norm, l0_mla_kv_norm, l0_mla_w_q_up, l0_mla_w_kv_up, l0_mla_w_o)),
        ("gdn", (l1_ada_w, l1_ada_b, l1_norm2, l1_mlp_w1, l1_mlp_w2),
         (l1_norm1, l1_gdn_w_in, l1_gdn_conv_w, l1_gdn_a_log, l1_gdn_dt_bias, l1_gdn_out_norm, l1_gdn_w_o)),
        ("mlstm", (l2_ada_w, l2_ada_b, l2_norm2, l2_mlp_w1, l2_mlp_w2),
         (l2_norm1, l2_mlstm_w_in, l2_mlstm_gate_b, l2_mlstm_out_norm, l2_mlstm_w_o)),
        ("mla", (l3_ada_w, l3_ada_b, l3_norm2, l3_mlp_w1, l3_mlp_w2),
         (l3_norm1, l3_mla_w_in, l3_mla_q_norm, l3_mla_kv_norm, l3_mla_w_q_up, l3_mla_w_kv_up, l3_mla_w_o)),
    )
    cond = jnp.concatenate([c, c_ctx[None, :], jnp.zeros((SUBLANES - batch - 1, d), F32)], 0)
    cos, sin = _rope_tables(seq)
    ctx_pad = ((0, batch * n_ctx), (0, 0))
    rope = (jnp.pad(jnp.tile(cos, (batch, 1)), ctx_pad, constant_values=1.0),
            jnp.pad(jnp.tile(sin, (batch, 1)), ctx_pad))
    hs = jnp.concatenate([x.reshape(lat_rows, d), ctx.reshape(batch * n_ctx, d)], 0)
    for li, (kind, (ada_w, ada_b, norm2, mlp_w1, mlp_w2), mixer_w) in enumerate(layers):
        ctx_out = li < len(layers) - 1
        mods = _ada(cond, ada_w, ada_b)[:batch + 1].reshape(batch + 1, 1, N_MOD * d)
        if kind == "mla":
            hs = _mla_layer(hs, mods, mixer_w, dims, n_ctx, ctx_out, rope)
        elif kind == "gdn":
            hs = _gdn_layer(hs, mods, mixer_w, dims, n_ctx)
        else:
            hs = _mlstm_layer(hs, mods, mixer_w, dims, n_ctx)
        hs = _mlp(hs, norm2, mods, mlp_w1.astype(BF16), mlp_w2.astype(BF16), dims, hs.shape[0], 512, 512)
    return _final_norm(hs, final_norm, lat_rows, 512).reshape(batch, seq, d)
```

```python
import functools
import math

import jax
import jax.numpy as jnp
from jax import lax
from jax.experimental import pallas as pl
from jax.experimental.pallas import tpu as pltpu

F32 = jnp.float32
BF16 = jnp.bfloat16

EPS = 1e-6
ROPE_THETA = 10000.0
GRID_W = 64
N_MOD = 6

MLA_HEADS = 16
MLA_Q_RANK = 768
MLA_KV_RANK = 512
MLA_NOPE = 128
MLA_ROPE = 64
MLA_V = 128
MLA_QK_PAD = 256
ATTN_UNROLL = 16

GDN_K_HEADS = 16
GDN_V_HEADS = 32
GDN_DK = 128
GDN_DV = 128
GDN_CONV = 5
GDN_GROUP = 4
GDN_CHAIN = 4

ML_HEADS = 8
ML_DQK = 128
ML_DV = 256
ML_GROUP = 8

CHUNK = 64
LANES = 128
SUBLANES = 8

VMEM_LIMIT = 52 * 1024 * 1024


def _cparams(sem):
    return pltpu.CompilerParams(dimension_semantics=sem, vmem_limit_bytes=VMEM_LIMIT)


def _dot(a, b):
    return jnp.dot(a, b, preferred_element_type=F32)


def _dot_nt(a, b):
    return lax.dot_general(a, b, (((1,), (1,)), ((), ())), preferred_element_type=F32)


def _dot_tn(a, b):
    return lax.dot_general(a, b, (((0,), (0,)), ((), ())), preferred_element_type=F32)


def _normmod(x, g, shift, scale):
    y = x * lax.rsqrt(jnp.mean(x * x, axis=-1, keepdims=True) + EPS) * g
    return y * (1.0 + scale) + shift


def _sigmoid(x):
    return 1.0 / (1.0 + jnp.exp(-x))


def _softplus(x):
    return jnp.maximum(x, 0.0) + jnp.log(1.0 + jnp.exp(-jnp.abs(x)))


def _log_sigmoid(x):
    return -_softplus(-x)


def _ada_kernel(c_ref, w_ref, b_ref, o_ref):
    c = c_ref[...]
    s = (c * _sigmoid(c)).astype(BF16)
    o_ref[...] = _dot(s, w_ref[...].astype(BF16)) + b_ref[...]


def _ada(cond, w, b):
    rows, d = cond.shape
    n = w.shape[1]
    tn = 1024
    return pl.pallas_call(
        _ada_kernel,
        out_shape=jax.ShapeDtypeStruct((rows, n), F32),
        grid=(n // tn,),
        in_specs=[pl.BlockSpec((rows, d), lambda j: (0, 0)),
                  pl.BlockSpec((d, tn), lambda j: (0, j)),
                  pl.BlockSpec((1, tn), lambda j: (0, j))],
        out_specs=pl.BlockSpec((rows, tn), lambda j: (0, j)),
        compiler_params=_cparams(("arbitrary",)),
        name="ada_mod",
    )(cond, w, b.reshape(1, n))


def _nm_matmul_kernel(x_ref, g_ref, sh_ref, sc_ref, w_ref, o_ref, u_ref):
    @pl.when(pl.program_id(1) == 0)
    def _():
        u_ref[...] = _normmod(x_ref[...], g_ref[...], sh_ref[0], sc_ref[0]).astype(BF16)

    o_ref[...] = _dot(u_ref[...], w_ref[...]).astype(o_ref.dtype)


def _mod_spec(dims, tm, kind, ngrid):
    d, seq, batch = dims
    if ngrid == 1:
        return pl.BlockSpec((1, 1, d), lambda i: (jnp.minimum(i * tm // seq, batch), 0, kind))
    return pl.BlockSpec((1, 1, d), lambda i, j: (jnp.minimum(i * tm // seq, batch), 0, kind))


def _nm_matmul(x, g, mods, kinds, w, dims, rows, tm, tn, out_dtype=F32, name="nm_matmul"):
    d = dims[0]
    n = w.shape[1]
    return pl.pallas_call(
        _nm_matmul_kernel,
        out_shape=jax.ShapeDtypeStruct((rows, n), out_dtype),
        grid=(rows // tm, n // tn),
        in_specs=[pl.BlockSpec((tm, d), lambda i, j: (i, 0)),
                  pl.BlockSpec((1, d), lambda i, j: (0, 0)),
                  _mod_spec(dims, tm, kinds[0], 2),
                  _mod_spec(dims, tm, kinds[1], 2),
                  pl.BlockSpec((d, tn), lambda i, j: (0, j))],
        out_specs=pl.BlockSpec((tm, tn), lambda i, j: (i, j)),
        scratch_shapes=[pltpu.VMEM((tm, d), BF16)],
        compiler_params=_cparams(("parallel", "arbitrary")),
        name=name,
    )(x, g.reshape(1, d), mods, mods, w)


def _mlp_kernel(x_ref, g_ref, sh_ref, sc_ref, gate_ref, w1_ref, w2_ref, o_ref, u_ref, acc_ref):
    k = pl.program_id(1)

    @pl.when(k == 0)
    def _():
        u_ref[...] = _normmod(x_ref[...], g_ref[...], sh_ref[0], sc_ref[0]).astype(BF16)
        acc_ref[...] = jnp.zeros_like(acc_ref)

    h = jnp.maximum(_dot(u_ref[...], w1_ref[...]), 0.0)
    acc_ref[...] += _dot((h * h).astype(BF16), w2_ref[...])

    @pl.when(k == pl.num_programs(1) - 1)
    def _():
        o_ref[...] = x_ref[...] + gate_ref[0] * acc_ref[...]


def _mlp(x, g, mods, w1, w2, dims, rows, tm, tk):
    d = dims[0]
    hidden = w1.shape[1]
    return pl.pallas_call(
        _mlp_kernel,
        out_shape=jax.ShapeDtypeStruct((rows, d), F32),
        grid=(rows // tm, hidden // tk),
        in_specs=[pl.BlockSpec((tm, d), lambda i, k: (i, 0)),
                  pl.BlockSpec((1, d), lambda i, k: (0, 0)),
                  _mod_spec(dims, tm, 3, 2),
                  _mod_spec(dims, tm, 4, 2),
                  _mod_spec(dims, tm, 5, 2),
                  pl.BlockSpec((d, tk), lambda i, k: (0, k)),
                  pl.BlockSpec((tk, d), lambda i, k: (k, 0))],
        out_specs=pl.BlockSpec((tm, d), lambda i, k: (i, 0)),
        scratch_shapes=[pltpu.VMEM((tm, d), BF16), pltpu.VMEM((tm, d), F32)],
        compiler_params=_cparams(("parallel", "arbitrary")),
        name="mlp",
    )(x, g.reshape(1, d), mods, mods, mods, w1, w2)


def _proj_res_kernel(*refs, n_lat_blocks):
    if n_lat_blocks is None:
        a_ref, res_ref, gate_ref, w_ref, o_ref = refs
        o_ref[...] = res_ref[...] + gate_ref[0] * _dot(a_ref[...], w_ref[...])
        return
    a_ref, ac_ref, res_ref, gate_ref, w_ref, o_ref = refs
    is_lat = pl.program_id(0) < n_lat_blocks

    @pl.when(is_lat)
    def _():
        o_ref[...] = res_ref[...] + gate_ref[0] * _dot(a_ref[...], w_ref[...])

    @pl.when(jnp.logical_not(is_lat))
    def _():
        o_ref[...] = res_ref[...] + gate_ref[0] * _dot(ac_ref[...], w_ref[...])


def _proj_res(a, a_ctx, res, mods, w, dims, rows, tm):
    d = dims[0]
    kdim = a.shape[1]
    nl = a.shape[0] // tm
    a_specs = [pl.BlockSpec((tm, kdim), lambda i: (jnp.minimum(i, nl - 1), 0))]
    a_args = [a]
    if a_ctx is not None:
        a_specs.append(pl.BlockSpec((tm, kdim), lambda i: (jnp.maximum(i - nl, 0), 0)))
        a_args.append(a_ctx)
    return pl.pallas_call(
        functools.partial(_proj_res_kernel, n_lat_blocks=None if a_ctx is None else nl),
        out_shape=jax.ShapeDtypeStruct((rows, d), F32),
        grid=(rows // tm,),
        in_specs=a_specs + [pl.BlockSpec((tm, d), lambda i: (i, 0)),
                            _mod_spec(dims, tm, 2, 1),
                            pl.BlockSpec((kdim, d), lambda i: (0, 0))],
        out_specs=pl.BlockSpec((tm, d), lambda i: (i, 0)),
        compiler_params=_cparams(("parallel",)),
        name="proj_res",
    )(*a_args, res, mods, w)


def _rope(x, cos, sin):
    lane = lax.broadcasted_iota(jnp.int32, x.shape, 1)
    partner = jnp.where(lane % 32 < 16, pltpu.roll(x, LANES - 16, 1), pltpu.roll(x, 16, 1))
    return x * cos + partner * sin


def _mla_up_kernel(c_ref, qn_ref, kvn_ref, wqn_ref, wqr_ref, wkn_ref, wv_ref, cos_ref, sin_ref,
                   q_ref, k_ref, v_ref):
    c = c_ref[...]
    cq = c[:, :MLA_Q_RANK]
    ckv = c[:, MLA_Q_RANK:MLA_Q_RANK + MLA_KV_RANK]
    kr = c[:, MLA_Q_RANK + MLA_KV_RANK:]
    cq = (cq * lax.rsqrt(jnp.mean(cq * cq, -1, keepdims=True) + EPS) * qn_ref[...]).astype(BF16)
    ckv = (ckv * lax.rsqrt(jnp.mean(ckv * ckv, -1, keepdims=True) + EPS) * kvn_ref[...]).astype(BF16)
    cos = cos_ref[...]
    sin = sin_ref[...]
    scale = (MLA_NOPE + MLA_ROPE) ** -0.5 * math.log2(math.e)
    qn = _dot(cq, wqn_ref[...]) * scale
    qr = _dot(cq, wqr_ref[...]) * scale
    kn = _dot(ckv, wkn_ref[...])
    v_ref[...] = _dot(ckv, wv_ref[...]).astype(v_ref.dtype)
    kr = _rope(kr, cos, sin).astype(k_ref.dtype)
    for h in range(MLA_HEADS):
        lo = h * MLA_QK_PAD
        sl = slice(h * LANES, (h + 1) * LANES)
        q_ref[:, lo:lo + LANES] = qn[:, sl].astype(q_ref.dtype)
        q_ref[:, lo + LANES:lo + 2 * LANES] = _rope(qr[:, sl], cos, sin).astype(q_ref.dtype)
        k_ref[:, lo:lo + LANES] = kn[:, sl].astype(k_ref.dtype)
        k_ref[:, lo + LANES:lo + 2 * LANES] = kr


def _mla_up(c, q_norm, kv_norm, wqn, wqr, wkn, wv, cos, sin, rows, tm):
    cw = c.shape[1]
    hq = MLA_HEADS * MLA_QK_PAD
    hv = MLA_HEADS * MLA_V
    full = lambda a: pl.BlockSpec(a.shape, lambda i: (0, 0))
    return pl.pallas_call(
        _mla_up_kernel,
        out_shape=(jax.ShapeDtypeStruct((rows, hq), BF16),
                   jax.ShapeDtypeStruct((rows, hq), BF16),
                   jax.ShapeDtypeStruct((rows, hv), BF16)),
        grid=(rows // tm,),
        in_specs=[pl.BlockSpec((tm, cw), lambda i: (i, 0)),
                  full(q_norm), full(kv_norm), full(wqn), full(wqr), full(wkn), full(wv),
                  pl.BlockSpec((tm, LANES), lambda i: (i, 0)),
                  pl.BlockSpec((tm, LANES), lambda i: (i, 0))],
        out_specs=(pl.BlockSpec((tm, hq), lambda i: (i, 0)),
                   pl.BlockSpec((tm, hq), lambda i: (i, 0)),
                   pl.BlockSpec((tm, hv), lambda i: (i, 0))),
        compiler_params=_cparams(("parallel",)),
        name="mla_up",
    )(c, q_norm, kv_norm, wqn, wqr, wkn, wv, cos, sin)


def _attn_kernel(*refs, n_lat, tk):
    if n_lat:
        q_ref, kl_ref, vl_ref, kc_ref, vc_ref, o_ref = refs
    else:
        q_ref, kc_ref, vc_ref, o_ref = refs
    q = q_ref[...]
    tq = q.shape[0]

    def step(k, v, carry):
        m, acc = carry
        s = _dot_nt(q, k)
        m_new = jnp.maximum(m, jnp.max(s, axis=-1, keepdims=True))
        p = jnp.exp2(s - m_new).astype(BF16)
        v_ext = jnp.concatenate([v, jnp.ones_like(v)], axis=1)
        return m_new, jnp.exp2(m - m_new) * acc + _dot(p, v_ext)

    carry = (jnp.full((tq, 1), -jnp.inf, F32), jnp.zeros((tq, 2 * MLA_V), F32))
    if n_lat:
        def body(i, carry):
            off = pl.multiple_of(i * tk, tk)
            return step(kl_ref[pl.ds(off, tk), :], vl_ref[pl.ds(off, tk), :], carry)

        carry = lax.fori_loop(0, n_lat // tk, body, carry, unroll=math.gcd(ATTN_UNROLL, n_lat // tk))
    m, acc = step(kc_ref[...], vc_ref[...], carry)
    o_ref[...] = (acc[:, :MLA_V] / acc[:, MLA_V:]).astype(o_ref.dtype)


def _attention(q, k, v, dims, n_ctx, tq, tk, ctx_queries):
    d, seq, batch = dims
    ctx_blk0 = batch * seq // n_ctx
    kc_spec = pl.BlockSpec((n_ctx, MLA_QK_PAD), lambda b, h, i: (ctx_blk0 + b, h))
    vc_spec = pl.BlockSpec((n_ctx, MLA_V), lambda b, h, i: (ctx_blk0 + b, h))
    if not ctx_queries:
        nq = seq // tq
        return pl.pallas_call(
            functools.partial(_attn_kernel, n_lat=seq, tk=tk),
            out_shape=jax.ShapeDtypeStruct((batch * seq, MLA_HEADS * MLA_V), BF16),
            grid=(batch, MLA_HEADS, nq),
            in_specs=[pl.BlockSpec((tq, MLA_QK_PAD), lambda b, h, i: (b * nq + i, h)),
                      pl.BlockSpec((seq, MLA_QK_PAD), lambda b, h, i: (b, h)),
                      pl.BlockSpec((seq, MLA_V), lambda b, h, i: (b, h)),
                      kc_spec, vc_spec],
            out_specs=pl.BlockSpec((tq, MLA_V), lambda b, h, i: (b * nq + i, h)),
            compiler_params=_cparams(("parallel", "parallel", "arbitrary")),
            name="mla_attention",
        )(q, k, v, k, v)
    return pl.pallas_call(
        functools.partial(_attn_kernel, n_lat=0, tk=tk),
        out_shape=jax.ShapeDtypeStruct((batch * n_ctx, MLA_HEADS * MLA_V), BF16),
        grid=(batch, MLA_HEADS, 1),
        in_specs=[pl.BlockSpec((n_ctx, MLA_QK_PAD), lambda b, h, i: (ctx_blk0 + b, h)), kc_spec, vc_spec],
        out_specs=pl.BlockSpec((n_ctx, MLA_V), lambda b, h, i: (b, h)),
        compiler_params=_cparams(("parallel", "parallel", "arbitrary")),
        name="mla_attention_ctx",
    )(q, k, v)


def _rope_tables(seq):
    t = jnp.arange(seq)
    row = (t // GRID_W).astype(F32)
    col = (t % GRID_W).astype(F32)
    axis_dim = MLA_ROPE // 2
    freqs = jnp.power(ROPE_THETA, -jnp.arange(0, axis_dim, 2, dtype=F32) / axis_dim)
    ar = row[:, None] * freqs[None, :]
    ac = col[:, None] * freqs[None, :]
    cr, sr, cc, sc = jnp.cos(ar), jnp.sin(ar), jnp.cos(ac), jnp.sin(ac)
    pad1 = jnp.ones((seq, LANES - MLA_ROPE), F32)
    pad0 = jnp.zeros((seq, LANES - MLA_ROPE), F32)
    cos = jnp.concatenate([cr, cr, cc, cc, pad1], -1)
    sin = jnp.concatenate([-sr, sr, -sc, sc, pad0], -1)
    return cos, sin


def _mla_layer(hs, mods, w, dims, n_ctx, ctx_out, rope):
    d, seq, batch = dims
    rows = hs.shape[0]
    lat_rows = batch * seq
    norm1, w_in, q_norm, kv_norm, w_q_up, w_kv_up, w_o = w
    c_w = MLA_Q_RANK + MLA_KV_RANK
    w_in_p = jnp.pad(w_in, ((0, 0), (0, LANES - MLA_ROPE))).astype(BF16)
    c = _nm_matmul(hs, norm1, mods, (0, 1), w_in_p, dims, rows, 256, w_in_p.shape[1], name="mla_w_in")
    wq = w_q_up.reshape(MLA_Q_RANK, MLA_HEADS, MLA_NOPE + MLA_ROPE)
    wqn = wq[:, :, :MLA_NOPE].reshape(MLA_Q_RANK, -1).astype(BF16)
    wqr = jnp.pad(wq[:, :, MLA_NOPE:], ((0, 0), (0, 0), (0, LANES - MLA_ROPE))).reshape(MLA_Q_RANK, -1).astype(BF16)
    wkv = w_kv_up.reshape(MLA_KV_RANK, MLA_HEADS, MLA_NOPE + MLA_V)
    wkn = wkv[:, :, :MLA_NOPE].reshape(MLA_KV_RANK, -1).astype(BF16)
    wv = wkv[:, :, MLA_NOPE:].reshape(MLA_KV_RANK, -1).astype(BF16)
    cos, sin = rope
    q, k, v = _mla_up(c, q_norm.reshape(1, -1), kv_norm.reshape(1, -1), wqn, wqr, wkn, wv, cos, sin, rows, 256)
    tq = min(512, seq)
    tk = min(512, seq)
    o = _attention(q, k, v, dims, n_ctx, tq, tk, False)
    o_ctx = _attention(q, k, v, dims, n_ctx, tq, tk, True) if ctx_out else None
    return _proj_res(o, o_ctx, hs, mods, w_o.astype(BF16), dims, rows if ctx_out else lat_rows, 256)


def _seq_edges(i, tm, seq, batch, n_ctx):
    r0 = i * tm
    lat = r0 < batch * seq
    rel = jnp.where(lat, r0 % seq, (r0 - batch * seq) % n_ctx)
    length = jnp.where(lat, seq, n_ctx)
    return rel == 0, rel + tm == length


def _gdn_conv_kernel(prev_ref, cur_ref, next_ref, w_ref, o_ref, ext_ref, *, tm, seq, batch, n_ctx, n_norm):
    i = pl.program_id(0)
    j = pl.program_id(1)
    first, last = _seq_edges(i, tm, seq, batch, n_ctx)
    ext_ref[0:SUBLANES, :] = jnp.where(first, 0.0, prev_ref[...])
    ext_ref[SUBLANES:SUBLANES + tm, :] = cur_ref[...]
    ext_ref[SUBLANES + tm:, :] = jnp.where(last, 0.0, next_ref[...])
    pad = GDN_CONV // 2
    y = None
    for t in range(GDN_CONV):
        term = w_ref[t:t + 1, :] * ext_ref[pl.ds(SUBLANES - pad + t, tm), :]
        y = term if y is None else y + term
    y = y * _sigmoid(y)

    @pl.when(j >= n_norm)
    def _():
        o_ref[...] = y

    @pl.when(j < n_norm)
    def _():
        sc = jnp.where(j < n_norm // 2, GDN_DK ** -0.5, 1.0)
        for h in range(y.shape[1] // GDN_DK):
            seg = y[:, h * GDN_DK:(h + 1) * GDN_DK]
            o_ref[:, h * GDN_DK:(h + 1) * GDN_DK] = seg * (lax.rsqrt(jnp.sum(seg * seg, -1, keepdims=True) + EPS) * sc)


def _gdn_conv(qkvz, conv_w, dims, n_ctx, rows, tm, tc):
    d, seq, batch = dims
    chans = conv_w.shape[1]
    n_norm = 2 * GDN_K_HEADS * GDN_DK // tc
    rb = tm // SUBLANES
    last_blk = rows // SUBLANES - 1
    return pl.pallas_call(
        functools.partial(_gdn_conv_kernel, tm=tm, seq=seq, batch=batch, n_ctx=n_ctx, n_norm=n_norm),
        out_shape=jax.ShapeDtypeStruct((rows, chans), F32),
        grid=(rows // tm, chans // tc),
        in_specs=[pl.BlockSpec((SUBLANES, tc), lambda i, j: (jnp.maximum(i * rb - 1, 0), j)),
                  pl.BlockSpec((tm, tc), lambda i, j: (i, j)),
                  pl.BlockSpec((SUBLANES, tc), lambda i, j: (jnp.minimum((i + 1) * rb, last_blk), j)),
                  pl.BlockSpec((GDN_CONV, tc), lambda i, j: (0, j))],
        out_specs=pl.BlockSpec((tm, tc), lambda i, j: (i, j)),
        scratch_shapes=[pltpu.VMEM((tm + 2 * SUBLANES, tc), F32)],
        compiler_params=_cparams(("parallel", "arbitrary")),
        name="gdn_conv",
    )(qkvz, qkvz, qkvz, conv_w)


def _gdn_gate_kernel(x_ref, a_ref, dt_ref, o_ref):
    x = x_ref[...]
    lane = lax.broadcasted_iota(jnp.int32, x.shape, 1)
    g = -jnp.exp(a_ref[...]) * _softplus(x + dt_ref[...])
    o_ref[...] = jnp.where(lane % (2 * GDN_V_HEADS) >= GDN_V_HEADS, g, _sigmoid(x))


def _gdn_gates(gates, a_log, dt_bias, rows, tm):
    zeros = jnp.zeros_like(a_log)
    a_row = jnp.stack([zeros, a_log], 1).reshape(1, LANES)
    dt_row = jnp.stack([zeros, dt_bias], 1).reshape(1, LANES)
    return pl.pallas_call(
        _gdn_gate_kernel,
        out_shape=jax.ShapeDtypeStruct((rows, LANES), F32),
        grid=(rows // tm,),
        in_specs=[pl.BlockSpec((tm, LANES), lambda i: (i, 0)),
                  pl.BlockSpec((1, LANES), lambda i: (0, 0)),
                  pl.BlockSpec((1, LANES), lambda i: (0, 0))],
        out_specs=pl.BlockSpec((tm, LANES), lambda i: (i, 0)),
        compiler_params=_cparams(("parallel",)),
        name="gdn_gates",
    )(gates, a_row, dt_row)


def _chunk_masks(d):
    row = lax.broadcasted_iota(jnp.int32, (CHUNK, 2 * CHUNK), 0)
    col = lax.broadcasted_iota(jnp.int32, (CHUNK, 2 * CHUNK), 1)
    rel = (row - col % CHUNK) * (1 - 2 * d)
    return rel >= 0, rel > 0, col < CHUNK, row == col - CHUNK


def _cumsum_tables(x, d):
    r2 = lax.broadcasted_iota(jnp.int32, (2 * CHUNK, CHUNK), 0) % CHUNK
    c2 = lax.broadcasted_iota(jnp.int32, (2 * CHUNK, CHUNK), 1)
    incl2 = jnp.where((r2 - c2) * (1 - 2 * d) >= 0, 1.0, 0.0)
    cs2 = jnp.dot(incl2, x, preferred_element_type=F32, precision=lax.Precision.HIGHEST)
    return cs2[:CHUNK], cs2.T


def _split_bf16(x):
    hi = x.astype(BF16)
    return hi, (x - hi.astype(F32)).astype(BF16)


def _block_diag_lhs(x, keep, top):
    return jnp.concatenate([jnp.where(keep & top, x, 0.0), jnp.where(keep & ~top, x, 0.0)], axis=1)


def _gdn_scan_kernel(qf_ref, kf_ref, vf_ref, bf_ref, gf_ref, qb_ref, kb_ref, vb_ref, bb_ref, gb_ref,
                     of_ref, ob_ref, s_ref):
    @pl.when(pl.program_id(2) == 0)
    def _():
        s_ref[...] = jnp.zeros_like(s_ref)

    rep = GDN_V_HEADS // GDN_K_HEADS
    n_heads = GDN_GROUP * rep
    streams = ((qf_ref, kf_ref, vf_ref, bf_ref, gf_ref, of_ref), (qb_ref, kb_ref, vb_ref, bb_ref, gb_ref, ob_ref))
    row4 = lax.broadcasted_iota(jnp.int32, (GDN_CHAIN * CHUNK, 2 * CHUNK), 0)
    col4 = lax.broadcasted_iota(jnp.int32, (GDN_CHAIN * CHUNK, 2 * CHUNK), 1)
    p_blk = (row4 // CHUNK) % 2 == col4 // CHUNK
    top = row4 < 2 * CHUNK

    heads = []
    chains = []
    for d, (q_ref, k_ref, v_ref, beta_ref, g_ref, o_ref) in enumerate(streams):
        incl, strict, left, eye_right = _chunk_masks(d)
        eye_left = jnp.logical_and(incl, jnp.logical_and(~strict, left))
        g_all = g_ref[...]
        gc, gct = _cumsum_tables(g_all, d)
        g_tot = jnp.sum(g_all, axis=0, keepdims=True)
        beta_all = beta_ref[...]
        for kh in range(GDN_GROUP):
            ksl = slice(kh * GDN_DK, (kh + 1) * GDN_DK)
            kk_b = k_ref[:, ksl].astype(BF16)
            gram = _dot_nt(jnp.concatenate([kk_b, q_ref[:, ksl].astype(BF16)], axis=0),
                           jnp.concatenate([kk_b, kk_b], axis=0))
            for r in range(rep):
                h = kh * rep + r
                beta = beta_all[:, h:h + 1]
                gcol = gc[:, h:h + 1]
                decay = jnp.exp(jnp.where(incl, gcol - gct[h:h + 1, :], -jnp.inf))
                n_mat = -jnp.where(strict, beta * gram[:CHUNK] * decay, 0.0)
                if h % 2 == 0:
                    ps = jnp.where(left, n_mat, jnp.where(eye_right, 1.0, 0.0))
                else:
                    ps = jnp.where(left, jnp.where(eye_left, 1.0, 0.0), n_mat)
                heads.append(dict(d=d, h=h, ksl=ksl, k_ref=k_ref, q_ref=q_ref, v_ref=v_ref, o_ref=o_ref,
                                  beta=beta, gcol=gcol, tail=g_tot[:, h:h + 1], ps=ps,
                                  attn=gram[CHUNK:] * decay))
        for c in range(n_heads // GDN_CHAIN):
            chains.append([hd for hd in heads if hd["d"] == d][c * GDN_CHAIN:(c + 1) * GDN_CHAIN])
    stacks = [jnp.concatenate([hd["ps"] for hd in ch], axis=0) for ch in chains]

    for _ in range(int(math.log2(CHUNK))):
        for ci in range(len(chains)):
            ps = stacks[ci]
            lhs_hi, lhs_lo = _split_bf16(_block_diag_lhs(ps, p_blk, top))
            ps_hi, ps_lo = _split_bf16(ps)
            wide = _dot(lhs_hi, jnp.concatenate([ps_hi, ps_lo], axis=1))
            stacks[ci] = (wide[:, :2 * CHUNK] + wide[:, 2 * CHUNK:] + _dot(lhs_lo, ps_hi)
                          + jnp.where(p_blk, 0.0, ps))

    for ci, ch in enumerate(chains):
        t_lhs = _block_diag_lhs(stacks[ci], ~p_blk, top).astype(BF16)
        rhs = []
        for j in (1, 0, 3, 2):
            hd = ch[j]
            kb = hd["k_ref"][:, hd["ksl"]] * hd["beta"]
            vsl = slice(hd["h"] * GDN_DV, (hd["h"] + 1) * GDN_DV)
            rhs.append(jnp.concatenate([(hd["v_ref"][:, vsl] * hd["beta"]).astype(BF16),
                                        (kb * jnp.exp(hd["gcol"])).astype(BF16)], axis=1))
        w = _dot(t_lhs, jnp.concatenate(rhs, axis=0))
        for j, hd in enumerate(ch):
            hd["value"] = w[j * CHUNK:(j + 1) * CHUNK, :GDN_DV]
            hd["k_cum"] = w[j * CHUNK:(j + 1) * CHUNK, GDN_DV:]

    for hd in heads:
        state = s_ref[hd["d"] * n_heads + hd["h"]]
        q_dec = hd["q_ref"][:, hd["ksl"]] * jnp.exp(hd["gcol"])
        ks_qs = _dot(jnp.concatenate([hd["k_cum"].astype(BF16), q_dec.astype(BF16)], axis=0), state.astype(BF16))
        hd["state"] = state
        hd["v_new"] = (hd["value"] - ks_qs[:CHUNK]).astype(BF16)
        hd["qs"] = ks_qs[CHUNK:]
    for ch in chains:
        a_lhs = _block_diag_lhs(jnp.concatenate([hd["attn"] for hd in ch], axis=0), p_blk, top).astype(BF16)
        o_attn = _dot(a_lhs, jnp.concatenate([hd["v_new"] for hd in ch], axis=0))
        for j, hd in enumerate(ch):
            hd["o_ref"][:, hd["h"] * GDN_DV:(hd["h"] + 1) * GDN_DV] = hd["qs"] + o_attn[j * CHUNK:(j + 1) * CHUNK]
    for hd in heads:
        k_tail = hd["k_ref"][:, hd["ksl"]] * jnp.exp(hd["tail"] - hd["gcol"])
        s_ref[hd["d"] * n_heads + hd["h"]] = (hd["state"] * jnp.exp(hd["tail"])
                                              + _dot_tn(k_tail.astype(BF16), hd["v_new"]))


def _scan_row_block(d, b, s, seq, batch, n_ctx):
    nc = n_ctx // CHUNK
    nl = seq // CHUNK
    ctx_pos = jnp.where(d == 0, s, nc - 1 - s)
    lat_pos = jnp.where(d == 0, s - nc, nl - 1 - (s - nc))
    return jnp.where(s < nc, (batch * seq + b * n_ctx) // CHUNK + ctx_pos, b * nl + lat_pos)


def _group_gates(x, n_heads, group, rows):
    ng = n_heads // group
    x = x.reshape(rows, 2, 2, ng, group).transpose(2, 1, 3, 0, 4)
    x = jnp.pad(x, ((0, 0), (0, 0), (0, 0), (0, 0), (0, LANES - group)))
    return x[0], x[1]


def _gdn_scan(act, beta, g, dims, n_ctx, rows):
    d_model, seq, batch = dims
    rep = GDN_V_HEADS // GDN_K_HEADS
    ng = GDN_K_HEADS // GDN_GROUP
    gw = GDN_GROUP * GDN_DK
    vw = GDN_GROUP * rep * GDN_DV
    k_blk0 = GDN_K_HEADS * GDN_DK // gw
    v_blk0 = 2 * GDN_K_HEADS * GDN_DK // vw
    steps = (n_ctx + seq) // CHUNK
    rb = functools.partial(_scan_row_block, seq=seq, batch=batch, n_ctx=n_ctx)
    in_specs = []
    for d in range(2):
        in_specs += [pl.BlockSpec((CHUNK, gw), lambda b, hg, s, d=d: (rb(d, b, s), hg)),
                     pl.BlockSpec((CHUNK, gw), lambda b, hg, s, d=d: (rb(d, b, s), k_blk0 + hg)),
                     pl.BlockSpec((CHUNK, vw), lambda b, hg, s, d=d: (rb(d, b, s), v_blk0 + hg)),
                     pl.BlockSpec((None, None, CHUNK, LANES), lambda b, hg, s, d=d: (d, hg, rb(d, b, s), 0)),
                     pl.BlockSpec((None, None, CHUNK, LANES), lambda b, hg, s, d=d: (d, hg, rb(d, b, s), 0))]
    out = jax.ShapeDtypeStruct((rows, GDN_V_HEADS * GDN_DV), F32)
    return pl.pallas_call(
        _gdn_scan_kernel,
        out_shape=(out, out),
        grid=(batch, ng, steps),
        in_specs=in_specs,
        out_specs=tuple(pl.BlockSpec((CHUNK, vw), lambda b, hg, s, d=d: (rb(d, b, s), hg)) for d in range(2)),
        scratch_shapes=[pltpu.VMEM((2 * GDN_GROUP * rep, GDN_DK, GDN_DV), F32)],
        compiler_params=_cparams(("parallel", "parallel", "arbitrary")),
        name="gdn_scan",
    )(act, act, act, beta, g, act, act, act, beta, g)


def _gdn_out_kernel(of_ref, ob_ref, z_ref, res_ref, gn_ref, gate_ref, w_ref, out_ref, acc_ref):
    k = pl.program_id(1)

    @pl.when(k == 0)
    def _():
        acc_ref[...] = jnp.zeros_like(acc_ref)

    o = of_ref[...] + ob_ref[...]
    z = z_ref[...]
    gn = gn_ref[...]
    parts = []
    for h in range(o.shape[1] // GDN_DV):
        seg = o[:, h * GDN_DV:(h + 1) * GDN_DV]
        zz = z[:, h * GDN_DV:(h + 1) * GDN_DV]
        seg = seg * lax.rsqrt(jnp.mean(seg * seg, -1, keepdims=True) + EPS) * gn
        parts.append((seg * (zz * _sigmoid(zz))).astype(BF16))
    acc_ref[...] += _dot(jnp.concatenate(parts, axis=1), w_ref[...])

    @pl.when(k == pl.num_programs(1) - 1)
    def _():
        out_ref[...] = res_ref[...] + gate_ref[0] * acc_ref[...]


def _gdn_out(o, qkvz, res, out_norm, mods, w_o, dims, rows, tm, tkk):
    d = dims[0]
    width = o[0].shape[1]
    z_blk0 = (qkvz.shape[1] - width) // tkk
    return pl.pallas_call(
        _gdn_out_kernel,
        out_shape=jax.ShapeDtypeStruct((rows, d), F32),
        grid=(rows // tm, width // tkk),
        in_specs=[pl.BlockSpec((tm, tkk), lambda i, k: (i, k)),
                  pl.BlockSpec((tm, tkk), lambda i, k: (i, k)),
                  pl.BlockSpec((tm, tkk), lambda i, k: (i, z_blk0 + k)),
                  pl.BlockSpec((tm, d), lambda i, k: (i, 0)),
                  pl.BlockSpec((1, GDN_DV), lambda i, k: (0, 0)),
                  _mod_spec(dims, tm, 2, 2),
                  pl.BlockSpec((tkk, d), lambda i, k: (k, 0))],
        out_specs=pl.BlockSpec((tm, d), lambda i, k: (i, 0)),
        scratch_shapes=[pltpu.VMEM((tm, d), F32)],
        compiler_params=_cparams(("parallel", "arbitrary")),
        name="gdn_out",
    )(o[0], o[1], qkvz, res, out_norm.reshape(1, GDN_DV), mods, w_o)


def _gdn_layer(hs, mods, w, dims, n_ctx):
    d, seq, batch = dims
    rows = hs.shape[0]
    norm1, w_in, conv_w, a_log, dt_bias, out_norm, w_o = w
    main_w = 2 * GDN_K_HEADS * GDN_DK + 2 * GDN_V_HEADS * GDN_DV
    qkvz = _nm_matmul(hs, norm1, mods, (0, 1), w_in[:, :main_w].astype(BF16), dims, rows, 512, 2048, name="gdn_w_in")
    gates = _nm_matmul(hs, norm1, mods, (0, 1), w_in[:, main_w:].astype(BF16), dims, rows, 512, LANES,
                       name="gdn_w_gates")
    act = _gdn_conv(qkvz, conv_w, dims, n_ctx, rows, 256, 1024)
    bg = _gdn_gates(gates, a_log, dt_bias, rows, 512)
    beta, g = _group_gates(bg, GDN_V_HEADS, GDN_GROUP * GDN_V_HEADS // GDN_K_HEADS, rows)
    o = _gdn_scan(act, beta, g, dims, n_ctx, rows)
    return _gdn_out(o, qkvz, hs, out_norm, mods, w_o.astype(BF16), dims, rows, 256, 1024)


def _ml_gate_kernel(x_ref, b_ref, o_ref):
    x = x_ref[...] + b_ref[...]
    lane = lax.broadcasted_iota(jnp.int32, x.shape, 1)
    o_ref[...] = jnp.where(lane % (2 * ML_HEADS) >= ML_HEADS, _log_sigmoid(x), x)


def _ml_gates(gates, gate_b, rows, tm):
    b_row = jnp.pad(gate_b.reshape(1, -1), ((0, 0), (0, LANES - 4 * ML_HEADS)))
    return pl.pallas_call(
        _ml_gate_kernel,
        out_shape=jax.ShapeDtypeStruct((rows, LANES), F32),
        grid=(rows // tm,),
        in_specs=[pl.BlockSpec((tm, LANES), lambda i: (i, 0)),
                  pl.BlockSpec((1, LANES), lambda i: (0, 0))],
        out_specs=pl.BlockSpec((tm, LANES), lambda i: (i, 0)),
        compiler_params=_cparams(("parallel",)),
        name="mlstm_gates",
    )(gates, b_row)


def _ml_scan_kernel(qf_ref, kf_ref, vf_ref, if_ref, ff_ref, qb_ref, kb_ref, vb_ref, ib_ref, fb_ref,
                    of_ref, ob_ref, c_ref, n_ref, m_ref):
    @pl.when(pl.program_id(2) == 0)
    def _():
        c_ref[...] = jnp.zeros_like(c_ref)
        n_ref[...] = jnp.zeros_like(n_ref)
        m_ref[...] = jnp.zeros_like(m_ref)

    streams = ((qf_ref, kf_ref, vf_ref, if_ref, ff_ref, of_ref), (qb_ref, kb_ref, vb_ref, ib_ref, fb_ref, ob_ref))
    row = lax.broadcasted_iota(jnp.int32, (CHUNK, CHUNK), 0)
    col = lax.broadcasted_iota(jnp.int32, (CHUNK, CHUNK), 1)

    heads = []
    for d, (q_ref, k_ref, v_ref, li_ref, lf_ref, o_ref) in enumerate(streams):
        incl = (row - col) * (1 - 2 * d) >= 0
        lf = lf_ref[...]
        li = li_ref[...]
        bc, bct = _cumsum_tables(lf, d)
        b_tot = jnp.sum(lf, axis=0, keepdims=True)
        lit = jnp.concatenate([li, li], axis=0).T
        for h in range(ML_GROUP):
            qh = q_ref[:, h * ML_DQK:(h + 1) * ML_DQK]
            kh = k_ref[:, h * ML_DQK:(h + 1) * ML_DQK] * (ML_DQK ** -0.5)
            bcol = bc[:, h:h + 1]
            b_last = b_tot[:, h:h + 1]
            d_mat = jnp.where(incl, bcol - bct[h:h + 1, :CHUNK] + lit[h:h + 1, :CHUNK], -jnp.inf)
            w_end = b_last - bcol + li[:, h:h + 1]
            heads.append(dict(idx=d * ML_GROUP + h, h=h, q_ref=q_ref, v_ref=v_ref, o_ref=o_ref, kh=kh,
                              bcol=bcol, b_last=b_last, d_mat=d_mat,
                              d_max=jnp.max(d_mat, axis=-1, keepdims=True),
                              qk=_dot_nt(qh.astype(BF16), kh.astype(BF16)),
                              w_end=w_end, w_end_max=jnp.max(w_end, axis=0, keepdims=True)))

    for hd in heads:
        hd["m"] = m_ref[hd["idx"]][0:1, 0:1]
        hd["c"] = c_ref[hd["idx"]]
        hd["nv"] = n_ref[hd["idx"]]
        hd["v"] = hd["v_ref"][:, hd["h"] * ML_DV:(hd["h"] + 1) * ML_DV].astype(BF16)
        inter = hd["bcol"] + hd["m"]
        hd["mt"] = jnp.maximum(hd["d_max"], inter)
        hd["p"] = jnp.exp(hd["d_mat"] - hd["mt"]) * hd["qk"]
        hd["a_in"] = jnp.exp(inter - hd["mt"])
    for hd in heads:
        qh = hd["q_ref"][:, hd["h"] * ML_DQK:(hd["h"] + 1) * ML_DQK]
        num = _dot(hd["p"].astype(BF16), hd["v"]) + hd["a_in"] * _dot(qh.astype(BF16), hd["c"].astype(BF16))
        den = (jnp.sum(hd["p"], axis=-1, keepdims=True)
               + hd["a_in"] * jnp.sum(qh * hd["nv"], axis=-1, keepdims=True))
        hd["o_ref"][:, hd["h"] * ML_DV:(hd["h"] + 1) * ML_DV] = num / jnp.maximum(jnp.abs(den), jnp.exp(-hd["mt"]))
    for hd in heads:
        m_new = jnp.maximum(hd["b_last"] + hd["m"], hd["w_end_max"])
        carry_decay = jnp.exp(hd["b_last"] + hd["m"] - m_new)
        kw = hd["kh"] * jnp.exp(hd["w_end"] - m_new)
        c_ref[hd["idx"]] = carry_decay * hd["c"] + _dot_tn(kw.astype(BF16), hd["v"])
        n_ref[hd["idx"]] = carry_decay * hd["nv"] + jnp.sum(kw, axis=0, keepdims=True)
        m_ref[hd["idx"]] = jnp.broadcast_to(m_new, m_ref.shape[1:])


def _ml_scan(qkvo, li, lf, dims, n_ctx, rows):
    d_model, seq, batch = dims
    ng = ML_HEADS // ML_GROUP
    qw = ML_GROUP * ML_DQK
    vw = ML_GROUP * ML_DV
    k_blk0 = ML_HEADS * ML_DQK // qw
    v_blk0 = 2 * ML_HEADS * ML_DQK // vw
    steps = (n_ctx + seq) // CHUNK
    rb = functools.partial(_scan_row_block, seq=seq, batch=batch, n_ctx=n_ctx)
    in_specs = []
    for d in range(2):
        in_specs += [pl.BlockSpec((CHUNK, qw), lambda b, hg, s, d=d: (rb(d, b, s), hg)),
                     pl.BlockSpec((CHUNK, qw), lambda b, hg, s, d=d: (rb(d, b, s), k_blk0 + hg)),
                     pl.BlockSpec((CHUNK, vw), lambda b, hg, s, d=d: (rb(d, b, s), v_blk0 + hg)),
                     pl.BlockSpec((None, None, CHUNK, LANES), lambda b, hg, s, d=d: (d, hg, rb(d, b, s), 0)),
                     pl.BlockSpec((None, None, CHUNK, LANES), lambda b, hg, s, d=d: (d, hg, rb(d, b, s), 0))]
    out = jax.ShapeDtypeStruct((rows, ML_HEADS * ML_DV), F32)
    return pl.pallas_call(
        _ml_scan_kernel,
        out_shape=(out, out),
        grid=(batch, ng, steps),
        in_specs=in_specs,
        out_specs=tuple(pl.BlockSpec((CHUNK, vw), lambda b, hg, s, d=d: (rb(d, b, s), hg)) for d in range(2)),
        scratch_shapes=[pltpu.VMEM((2 * ML_GROUP, ML_DQK, ML_DV), F32),
                        pltpu.VMEM((2 * ML_GROUP, 1, ML_DQK), F32),
                        pltpu.VMEM((2 * ML_GROUP, SUBLANES, LANES), F32)],
        compiler_params=_cparams(("parallel", "parallel", "arbitrary")),
        name="mlstm_scan",
    )(qkvo, qkvo, qkvo, li, lf, qkvo, qkvo, qkvo, li, lf)


def _ml_out_kernel(hf_ref, hb_ref, og_ref, res_ref, gn_ref, gate_ref, w_ref, out_ref):
    hs = hf_ref[...] + hb_ref[...]
    og = og_ref[...]
    gn = gn_ref[...]
    parts = []
    for h in range(ML_HEADS):
        sl = slice(h * ML_DV, (h + 1) * ML_DV)
        seg = hs[:, sl]
        seg = seg * lax.rsqrt(jnp.mean(seg * seg, -1, keepdims=True) + EPS) * gn[:, sl]
        parts.append((seg * _sigmoid(og[:, sl])).astype(BF16))
    out_ref[...] = res_ref[...] + gate_ref[0] * _dot(jnp.concatenate(parts, axis=1), w_ref[...])


def _ml_out(hh, qkvo, res, out_norm, mods, w_o, dims, rows, tm):
    d = dims[0]
    width = hh[0].shape[1]
    og_blk = (qkvo.shape[1] - width) // width
    return pl.pallas_call(
        _ml_out_kernel,
        out_shape=jax.ShapeDtypeStruct((rows, d), F32),
        grid=(rows // tm,),
        in_specs=[pl.BlockSpec((tm, width), lambda i: (i, 0)),
                  pl.BlockSpec((tm, width), lambda i: (i, 0)),
                  pl.BlockSpec((tm, width), lambda i: (i, og_blk)),
                  pl.BlockSpec((tm, d), lambda i: (i, 0)),
                  pl.BlockSpec((1, width), lambda i: (0, 0)),
                  _mod_spec(dims, tm, 2, 1),
                  pl.BlockSpec((width, d), lambda i: (0, 0))],
        out_specs=pl.BlockSpec((tm, d), lambda i: (i, 0)),
        compiler_params=_cparams(("parallel",)),
        name="mlstm_out",
    )(hh[0], hh[1], qkvo, res, out_norm.reshape(1, width), mods, w_o)


def _mlstm_layer(hs, mods, w, dims, n_ctx):
    d, seq, batch = dims
    rows = hs.shape[0]
    norm1, w_in, gate_b, out_norm, w_o = w
    main_w = 2 * ML_HEADS * ML_DQK + 2 * ML_HEADS * ML_DV
    qkvo = _nm_matmul(hs, norm1, mods, (0, 1), w_in[:, :main_w].astype(BF16), dims, rows, 512, 2048,
                      name="mlstm_w_in")
    w_g = jnp.pad(w_in[:, main_w:], ((0, 0), (0, LANES - 4 * ML_HEADS))).astype(BF16)
    gates = _nm_matmul(hs, norm1, mods, (0, 1), w_g, dims, rows, 512, LANES, name="mlstm_w_gates")
    lg = _ml_gates(gates, gate_b, rows, 512)
    li, lf = _group_gates(lg[:, :4 * ML_HEADS], ML_HEADS, ML_GROUP, rows)
    hh = _ml_scan(qkvo, li, lf, dims, n_ctx, rows)
    return _ml_out(hh, qkvo, hs, out_norm, mods, w_o.astype(BF16), dims, rows, 256)


def _final_norm_kernel(x_ref, g_ref, o_ref):
    x = x_ref[...]
    o_ref[...] = x * lax.rsqrt(jnp.mean(x * x, -1, keepdims=True) + EPS) * g_ref[...]


def _final_norm(x, g, rows, tm):
    d = x.shape[1]
    return pl.pallas_call(
        _final_norm_kernel,
        out_shape=jax.ShapeDtypeStruct((rows, d), F32),
        grid=(rows // tm,),
        in_specs=[pl.BlockSpec((tm, d), lambda i: (i, 0)),
                  pl.BlockSpec((1, d), lambda i: (0, 0))],
        out_specs=pl.BlockSpec((tm, d), lambda i: (i, 0)),
        compiler_params=_cparams(("parallel",)),
        name="final_norm",
    )(x, g.reshape(1, d))


def kernel(x, c, ctx, c_ctx, l0_ada_w, l0_ada_b, l0_norm1, l0_mla_w_in, l0_mla_q_norm, l0_mla_kv_norm, l0_mla_w_q_up, l0_mla_w_kv_up, l0_mla_w_o, l0_norm2, l0_mlp_w1, l0_mlp_w2, l1_ada_w, l1_ada_b, l1_norm1, l1_gdn_w_in, l1_gdn_conv_w, l1_gdn_a_log, l1_gdn_dt_bias, l1_gdn_out_norm, l1_gdn_w_o, l1_norm2, l1_mlp_w1, l1_mlp_w2, l2_ada_w, l2_ada_b, l2_norm1, l2_mlstm_w_in, l2_mlstm_gate_b, l2_mlstm_out_norm, l2_mlstm_w_o, l2_norm2, l2_mlp_w1, l2_mlp_w2, l3_ada_w, l3_ada_b, l3_norm1, l3_mla_w_in, l3_mla_q_norm, l3_mla_kv_norm, l3_mla_w_q_up, l3_mla_w_kv_up, l3_mla_w_o, l3_norm2, l3_mlp_w1, l3_mlp_w2, final_norm):
    batch, seq, d = x.shape
    n_ctx = ctx.shape[1]
    dims = (d, seq, batch)
    lat_rows = batch * seq
    rows = lat_rows + batch * n_ctx
    layers = (
        ("mla", (l0_ada_w, l0_ada_b, l0_norm2, l0_mlp_w1, l0_mlp_w2),
         (l0_norm1, l0_mla_w_in, l0_mla_q_norm, l0_mla_kv_norm, l0_mla_w_q_up, l0_mla_w_kv_up, l0_mla_w_o)),
        ("gdn", (l1_ada_w, l1_ada_b, l1_norm2, l1_mlp_w1, l1_mlp_w2),
         (l1_norm1, l1_gdn_w_in, l1_gdn_conv_w, l1_gdn_a_log, l1_gdn_dt_bias, l1_gdn_out_norm, l1_gdn_w_o)),
        ("mlstm", (l2_ada_w, l2_ada_b, l2_norm2, l2_mlp_w1, l2_mlp_w2),
         (l2_norm1, l2_mlstm_w_in, l2_mlstm_gate_b, l2_mlstm_out_norm, l2_mlstm_w_o)),
        ("mla", (l3_ada_w, l3_ada_b, l3_norm2, l3_mlp_w1, l3_mlp_w2),
         (l3_norm1, l3_mla_w_in, l3_mla_q_norm, l3_mla_kv_norm, l3_mla_w_q_up, l3_mla_w_kv_up, l3_mla_w_o)),
    )
    cond = jnp.concatenate([c, c_ctx[None, :], jnp.zeros((SUBLANES - batch - 1, d), F32)], 0)
    cos, sin = _rope_tables(seq)
    ctx_pad = ((0, batch * n_ctx), (0, 0))
    rope = (jnp.pad(jnp.tile(cos, (batch, 1)), ctx_pad, constant_values=1.0),
            jnp.pad(jnp.tile(sin, (batch, 1)), ctx_pad))
    hs = jnp.concatenate([x.reshape(lat_rows, d), ctx.reshape(batch * n_ctx, d)], 0)
    for li, (kind, (ada_w, ada_b, norm2, mlp_w1, mlp_w2), mixer_w) in enumerate(layers):
        ctx_out = li < len(layers) - 1
        mods = _ada(cond, ada_w, ada_b)[:batch + 1].reshape(batch + 1, 1, N_MOD * d)
        if kind == "mla":
            hs = _mla_layer(hs, mods, mixer_w, dims, n_ctx, ctx_out, rope)
        elif kind == "gdn":
            hs = _gdn_layer(hs, mods, mixer_w, dims, n_ctx)
        else:
            hs = _mlstm_layer(hs, mods, mixer_w, dims, n_ctx)
        hs = _mlp(hs, norm2, mods, mlp_w1.astype(BF16), mlp_w2.astype(BF16), dims, hs.shape[0], 512, 1024)
    return _final_norm(hs, final_norm, lat_rows, 512).reshape(batch, seq, d)
```

```python
import functools
import math

import jax
import jax.numpy as jnp
from jax import lax
from jax.experimental import pallas as pl
from jax.experimental.pallas import tpu as pltpu

F32 = jnp.float32
BF16 = jnp.bfloat16

EPS = 1e-6
ROPE_THETA = 10000.0
GRID_W = 64
N_MOD = 6

MLA_HEADS = 16
MLA_Q_RANK = 768
MLA_KV_RANK = 512
MLA_NOPE = 128
MLA_ROPE = 64
MLA_V = 128
MLA_QK_PAD = 256
ATTN_UNROLL = 16

GDN_K_HEADS = 16
GDN_V_HEADS = 32
GDN_DK = 128
GDN_DV = 128
GDN_CONV = 5
GDN_GROUP = 4
GDN_CHAIN = 4

ML_HEADS = 8
ML_DQK = 128
ML_DV = 256
ML_GROUP = 8

CHUNK = 64
LANES = 128
SUBLANES = 8

VMEM_LIMIT = 52 * 1024 * 1024


def _cparams(sem):
    return pltpu.CompilerParams(dimension_semantics=sem, vmem_limit_bytes=VMEM_LIMIT)


def _dot(a, b):
    return jnp.dot(a, b, preferred_element_type=F32)


def _dot_nt(a, b):
    return lax.dot_general(a, b, (((1,), (1,)), ((), ())), preferred_element_type=F32)


def _dot_tn(a, b):
    return lax.dot_general(a, b, (((0,), (0,)), ((), ())), preferred_element_type=F32)


def _normmod(x, g, shift, scale):
    y = x * lax.rsqrt(jnp.mean(x * x, axis=-1, keepdims=True) + EPS) * g
    return y * (1.0 + scale) + shift


def _sigmoid(x):
    return 1.0 / (1.0 + jnp.exp(-x))


def _softplus(x):
    return jnp.maximum(x, 0.0) + jnp.log(1.0 + jnp.exp(-jnp.abs(x)))


def _log_sigmoid(x):
    return -_softplus(-x)


def _ada_kernel(c_ref, w_ref, b_ref, o_ref):
    c = c_ref[...]
    s = (c * _sigmoid(c)).astype(BF16)
    o_ref[...] = _dot(s, w_ref[...].astype(BF16)) + b_ref[...]


def _ada(cond, w, b):
    rows, d = cond.shape
    n = w.shape[1]
    tn = 1024
    return pl.pallas_call(
        _ada_kernel,
        out_shape=jax.ShapeDtypeStruct((rows, n), F32),
        grid=(n // tn,),
        in_specs=[pl.BlockSpec((rows, d), lambda j: (0, 0)),
                  pl.BlockSpec((d, tn), lambda j: (0, j)),
                  pl.BlockSpec((1, tn), lambda j: (0, j))],
        out_specs=pl.BlockSpec((rows, tn), lambda j: (0, j)),
        compiler_params=_cparams(("arbitrary",)),
        name="ada_mod",
    )(cond, w, b.reshape(1, n))


def _nm_matmul_kernel(x_ref, g_ref, sh_ref, sc_ref, w_ref, o_ref, u_ref):
    @pl.when(pl.program_id(1) == 0)
    def _():
        u_ref[...] = _normmod(x_ref[...], g_ref[...], sh_ref[0], sc_ref[0]).astype(BF16)

    o_ref[...] = _dot(u_ref[...], w_ref[...]).astype(o_ref.dtype)


def _mod_spec(dims, tm, kind, ngrid):
    d, seq, batch = dims
    if ngrid == 1:
        return pl.BlockSpec((1, 1, d), lambda i: (jnp.minimum(i * tm // seq, batch), 0, kind))
    return pl.BlockSpec((1, 1, d), lambda i, j: (jnp.minimum(i * tm // seq, batch), 0, kind))


def _nm_matmul(x, g, mods, kinds, w, dims, rows, tm, tn, out_dtype=F32, name="nm_matmul"):
    d = dims[0]
    n = w.shape[1]
    return pl.pallas_call(
        _nm_matmul_kernel,
        out_shape=jax.ShapeDtypeStruct((rows, n), out_dtype),
        grid=(rows // tm, n // tn),
        in_specs=[pl.BlockSpec((tm, d), lambda i, j: (i, 0)),
                  pl.BlockSpec((1, d), lambda i, j: (0, 0)),
                  _mod_spec(dims, tm, kinds[0], 2),
                  _mod_spec(dims, tm, kinds[1], 2),
                  pl.BlockSpec((d, tn), lambda i, j: (0, j))],
        out_specs=pl.BlockSpec((tm, tn), lambda i, j: (i, j)),
        scratch_shapes=[pltpu.VMEM((tm, d), BF16)],
        compiler_params=_cparams(("parallel", "arbitrary")),
        name=name,
    )(x, g.reshape(1, d), mods, mods, w)


def _mlp_kernel(x_ref, g_ref, sh_ref, sc_ref, gate_ref, w1_ref, w2_ref, o_ref, u_ref, acc_ref):
    k = pl.program_id(1)

    @pl.when(k == 0)
    def _():
        u_ref[...] = _normmod(x_ref[...], g_ref[...], sh_ref[0], sc_ref[0]).astype(BF16)
        acc_ref[...] = jnp.zeros_like(acc_ref)

    h = jnp.maximum(_dot(u_ref[...], w1_ref[...]), 0.0)
    acc_ref[...] += _dot((h * h).astype(BF16), w2_ref[...])

    @pl.when(k == pl.num_programs(1) - 1)
    def _():
        o_ref[...] = x_ref[...] + gate_ref[0] * acc_ref[...]


def _mlp(x, g, mods, w1, w2, dims, rows, tm, tk):
    d = dims[0]
    hidden = w1.shape[1]
    return pl.pallas_call(
        _mlp_kernel,
        out_shape=jax.ShapeDtypeStruct((rows, d), F32),
        grid=(rows // tm, hidden // tk),
        in_specs=[pl.BlockSpec((tm, d), lambda i, k: (i, 0)),
                  pl.BlockSpec((1, d), lambda i, k: (0, 0)),
                  _mod_spec(dims, tm, 3, 2),
                  _mod_spec(dims, tm, 4, 2),
                  _mod_spec(dims, tm, 5, 2),
                  pl.BlockSpec((d, tk), lambda i, k: (0, k)),
                  pl.BlockSpec((tk, d), lambda i, k: (k, 0))],
        out_specs=pl.BlockSpec((tm, d), lambda i, k: (i, 0)),
        scratch_shapes=[pltpu.VMEM((tm, d), BF16), pltpu.VMEM((tm, d), F32)],
        compiler_params=_cparams(("parallel", "arbitrary")),
        name="mlp",
    )(x, g.reshape(1, d), mods, mods, mods, w1, w2)


def _proj_res_kernel(*refs, n_lat_blocks):
    if n_lat_blocks is None:
        a_ref, res_ref, gate_ref, w_ref, o_ref = refs
        o_ref[...] = res_ref[...] + gate_ref[0] * _dot(a_ref[...], w_ref[...])
        return
    a_ref, ac_ref, res_ref, gate_ref, w_ref, o_ref = refs
    is_lat = pl.program_id(0) < n_lat_blocks

    @pl.when(is_lat)
    def _():
        o_ref[...] = res_ref[...] + gate_ref[0] * _dot(a_ref[...], w_ref[...])

    @pl.when(jnp.logical_not(is_lat))
    def _():
        o_ref[...] = res_ref[...] + gate_ref[0] * _dot(ac_ref[...], w_ref[...])


def _proj_res(a, a_ctx, res, mods, w, dims, rows, tm):
    d = dims[0]
    kdim = a.shape[1]
    nl = a.shape[0] // tm
    a_specs = [pl.BlockSpec((tm, kdim), lambda i: (jnp.minimum(i, nl - 1), 0))]
    a_args = [a]
    if a_ctx is not None:
        a_specs.append(pl.BlockSpec((tm, kdim), lambda i: (jnp.maximum(i - nl, 0), 0)))
        a_args.append(a_ctx)
    return pl.pallas_call(
        functools.partial(_proj_res_kernel, n_lat_blocks=None if a_ctx is None else nl),
        out_shape=jax.ShapeDtypeStruct((rows, d), F32),
        grid=(rows // tm,),
        in_specs=a_specs + [pl.BlockSpec((tm, d), lambda i: (i, 0)),
                            _mod_spec(dims, tm, 2, 1),
                            pl.BlockSpec((kdim, d), lambda i: (0, 0))],
        out_specs=pl.BlockSpec((tm, d), lambda i: (i, 0)),
        compiler_params=_cparams(("parallel",)),
        name="proj_res",
    )(*a_args, res, mods, w)


def _rope(x, cos, sin):
    lane = lax.broadcasted_iota(jnp.int32, x.shape, 1)
    partner = jnp.where(lane % 32 < 16, pltpu.roll(x, LANES - 16, 1), pltpu.roll(x, 16, 1))
    return x * cos + partner * sin


def _mla_up_kernel(c_ref, qn_ref, kvn_ref, wqn_ref, wqr_ref, wkn_ref, wv_ref, cos_ref, sin_ref,
                   q_ref, k_ref, v_ref):
    c = c_ref[...]
    cq = c[:, :MLA_Q_RANK]
    ckv = c[:, MLA_Q_RANK:MLA_Q_RANK + MLA_KV_RANK]
    kr = c[:, MLA_Q_RANK + MLA_KV_RANK:]
    cq = (cq * lax.rsqrt(jnp.mean(cq * cq, -1, keepdims=True) + EPS) * qn_ref[...]).astype(BF16)
    ckv = (ckv * lax.rsqrt(jnp.mean(ckv * ckv, -1, keepdims=True) + EPS) * kvn_ref[...]).astype(BF16)
    cos = cos_ref[...]
    sin = sin_ref[...]
    scale = (MLA_NOPE + MLA_ROPE) ** -0.5 * math.log2(math.e)
    qn = _dot(cq, wqn_ref[...]) * scale
    qr = _dot(cq, wqr_ref[...]) * scale
    kn = _dot(ckv, wkn_ref[...])
    v_ref[...] = _dot(ckv, wv_ref[...]).astype(v_ref.dtype)
    kr = _rope(kr, cos, sin).astype(k_ref.dtype)
    for h in range(MLA_HEADS):
        lo = h * MLA_QK_PAD
        sl = slice(h * LANES, (h + 1) * LANES)
        q_ref[:, lo:lo + LANES] = qn[:, sl].astype(q_ref.dtype)
        q_ref[:, lo + LANES:lo + 2 * LANES] = _rope(qr[:, sl], cos, sin).astype(q_ref.dtype)
        k_ref[:, lo:lo + LANES] = kn[:, sl].astype(k_ref.dtype)
        k_ref[:, lo + LANES:lo + 2 * LANES] = kr


def _mla_up(c, q_norm, kv_norm, wqn, wqr, wkn, wv, cos, sin, rows, tm):
    cw = c.shape[1]
    hq = MLA_HEADS * MLA_QK_PAD
    hv = MLA_HEADS * MLA_V
    full = lambda a: pl.BlockSpec(a.shape, lambda i: (0, 0))
    return pl.pallas_call(
        _mla_up_kernel,
        out_shape=(jax.ShapeDtypeStruct((rows, hq), BF16),
                   jax.ShapeDtypeStruct((rows, hq), BF16),
                   jax.ShapeDtypeStruct((rows, hv), BF16)),
        grid=(rows // tm,),
        in_specs=[pl.BlockSpec((tm, cw), lambda i: (i, 0)),
                  full(q_norm), full(kv_norm), full(wqn), full(wqr), full(wkn), full(wv),
                  pl.BlockSpec((tm, LANES), lambda i: (i, 0)),
                  pl.BlockSpec((tm, LANES), lambda i: (i, 0))],
        out_specs=(pl.BlockSpec((tm, hq), lambda i: (i, 0)),
                   pl.BlockSpec((tm, hq), lambda i: (i, 0)),
                   pl.BlockSpec((tm, hv), lambda i: (i, 0))),
        compiler_params=_cparams(("parallel",)),
        name="mla_up",
    )(c, q_norm, kv_norm, wqn, wqr, wkn, wv, cos, sin)


def _attn_kernel(*refs, n_lat, tk):
    if n_lat:
        q_ref, kl_ref, vl_ref, kc_ref, vc_ref, o_ref = refs
    else:
        q_ref, kc_ref, vc_ref, o_ref = refs
    q = q_ref[...]
    tq = q.shape[0]

    def step(k, v, carry):
        m, acc = carry
        s = _dot_nt(q, k)
        m_new = jnp.maximum(m, jnp.max(s, axis=-1, keepdims=True))
        p = jnp.exp2(s - m_new).astype(BF16)
        v_ext = jnp.concatenate([v, jnp.ones_like(v)], axis=1)
        return m_new, jnp.exp2(m - m_new) * acc + _dot(p, v_ext)

    carry = (jnp.full((tq, 1), -jnp.inf, F32), jnp.zeros((tq, 2 * MLA_V), F32))
    if n_lat:
        def body(i, carry):
            off = pl.multiple_of(i * tk, tk)
            return step(kl_ref[pl.ds(off, tk), :], vl_ref[pl.ds(off, tk), :], carry)

        carry = lax.fori_loop(0, n_lat // tk, body, carry, unroll=math.gcd(ATTN_UNROLL, n_lat // tk))
    m, acc = step(kc_ref[...], vc_ref[...], carry)
    o_ref[...] = (acc[:, :MLA_V] / acc[:, MLA_V:]).astype(o_ref.dtype)


def _attention(q, k, v, dims, n_ctx, tq, tk, ctx_queries):
    d, seq, batch = dims
    ctx_blk0 = batch * seq // n_ctx
    kc_spec = pl.BlockSpec((n_ctx, MLA_QK_PAD), lambda b, h, i: (ctx_blk0 + b, h))
    vc_spec = pl.BlockSpec((n_ctx, MLA_V), lambda b, h, i: (ctx_blk0 + b, h))
    if not ctx_queries:
        nq = seq // tq
        return pl.pallas_call(
            functools.partial(_attn_kernel, n_lat=seq, tk=tk),
            out_shape=jax.ShapeDtypeStruct((batch * seq, MLA_HEADS * MLA_V), BF16),
            grid=(batch, MLA_HEADS, nq),
            in_specs=[pl.BlockSpec((tq, MLA_QK_PAD), lambda b, h, i: (b * nq + i, h)),
                      pl.BlockSpec((seq, MLA_QK_PAD), lambda b, h, i: (b, h)),
                      pl.BlockSpec((seq, MLA_V), lambda b, h, i: (b, h)),
                      kc_spec, vc_spec],
            out_specs=pl.BlockSpec((tq, MLA_V), lambda b, h, i: (b * nq + i, h)),
            compiler_params=_cparams(("parallel", "parallel", "arbitrary")),
            name="mla_attention",
        )(q, k, v, k, v)
    return pl.pallas_call(
        functools.partial(_attn_kernel, n_lat=0, tk=tk),
        out_shape=jax.ShapeDtypeStruct((batch * n_ctx, MLA_HEADS * MLA_V), BF16),
        grid=(batch, MLA_HEADS, 1),
        in_specs=[pl.BlockSpec((n_ctx, MLA_QK_PAD), lambda b, h, i: (ctx_blk0 + b, h)), kc_spec, vc_spec],
        out_specs=pl.BlockSpec((n_ctx, MLA_V), lambda b, h, i: (b, h)),
        compiler_params=_cparams(("parallel", "parallel", "arbitrary")),
        name="mla_attention_ctx",
    )(q, k, v)


def _rope_tables(seq):
    t = jnp.arange(seq)
    row = (t // GRID_W).astype(F32)
    col = (t % GRID_W).astype(F32)
    axis_dim = MLA_ROPE // 2
    freqs = jnp.power(ROPE_THETA, -jnp.arange(0, axis_dim, 2, dtype=F32) / axis_dim)
    ar = row[:, None] * freqs[None, :]
    ac = col[:, None] * freqs[None, :]
    cr, sr, cc, sc = jnp.cos(ar), jnp.sin(ar), jnp.cos(ac), jnp.sin(ac)
    pad1 = jnp.ones((seq, LANES - MLA_ROPE), F32)
    pad0 = jnp.zeros((seq, LANES - MLA_ROPE), F32)
    cos = jnp.concatenate([cr, cr, cc, cc, pad1], -1)
    sin = jnp.concatenate([-sr, sr, -sc, sc, pad0], -1)
    return cos, sin


def _mla_layer(hs, mods, w, dims, n_ctx, ctx_out, rope):
    d, seq, batch = dims
    rows = hs.shape[0]
    lat_rows = batch * seq
    norm1, w_in, q_norm, kv_norm, w_q_up, w_kv_up, w_o = w
    c_w = MLA_Q_RANK + MLA_KV_RANK
    w_in_p = jnp.pad(w_in, ((0, 0), (0, LANES - MLA_ROPE))).astype(BF16)
    c = _nm_matmul(hs, norm1, mods, (0, 1), w_in_p, dims, rows, 256, w_in_p.shape[1], name="mla_w_in")
    wq = w_q_up.reshape(MLA_Q_RANK, MLA_HEADS, MLA_NOPE + MLA_ROPE)
    wqn = wq[:, :, :MLA_NOPE].reshape(MLA_Q_RANK, -1).astype(BF16)
    wqr = jnp.pad(wq[:, :, MLA_NOPE:], ((0, 0), (0, 0), (0, LANES - MLA_ROPE))).reshape(MLA_Q_RANK, -1).astype(BF16)
    wkv = w_kv_up.reshape(MLA_KV_RANK, MLA_HEADS, MLA_NOPE + MLA_V)
    wkn = wkv[:, :, :MLA_NOPE].reshape(MLA_KV_RANK, -1).astype(BF16)
    wv = wkv[:, :, MLA_NOPE:].reshape(MLA_KV_RANK, -1).astype(BF16)
    cos, sin = rope
    q, k, v = _mla_up(c, q_norm.reshape(1, -1), kv_norm.reshape(1, -1), wqn, wqr, wkn, wv, cos, sin, rows, 256)
    tq = min(512, seq)
    tk = min(512, seq)
    o = _attention(q, k, v, dims, n_ctx, tq, tk, False)
    o_ctx = _attention(q, k, v, dims, n_ctx, tq, tk, True) if ctx_out else None
    return _proj_res(o, o_ctx, hs, mods, w_o.astype(BF16), dims, rows if ctx_out else lat_rows, 512)


def _seq_edges(i, tm, seq, batch, n_ctx):
    r0 = i * tm
    lat = r0 < batch * seq
    rel = jnp.where(lat, r0 % seq, (r0 - batch * seq) % n_ctx)
    length = jnp.where(lat, seq, n_ctx)
    return rel == 0, rel + tm == length


def _gdn_conv_kernel(prev_ref, cur_ref, next_ref, w_ref, o_ref, ext_ref, *, tm, seq, batch, n_ctx, n_norm):
    i = pl.program_id(0)
    j = pl.program_id(1)
    first, last = _seq_edges(i, tm, seq, batch, n_ctx)
    ext_ref[0:SUBLANES, :] = jnp.where(first, 0.0, prev_ref[...])
    ext_ref[SUBLANES:SUBLANES + tm, :] = cur_ref[...]
    ext_ref[SUBLANES + tm:, :] = jnp.where(last, 0.0, next_ref[...])
    pad = GDN_CONV // 2
    y = None
    for t in range(GDN_CONV):
        term = w_ref[t:t + 1, :] * ext_ref[pl.ds(SUBLANES - pad + t, tm), :]
        y = term if y is None else y + term
    y = y * _sigmoid(y)

    @pl.when(j >= n_norm)
    def _():
        o_ref[...] = y

    @pl.when(j < n_norm)
    def _():
        sc = jnp.where(j < n_norm // 2, GDN_DK ** -0.5, 1.0)
        for h in range(y.shape[1] // GDN_DK):
            seg = y[:, h * GDN_DK:(h + 1) * GDN_DK]
            o_ref[:, h * GDN_DK:(h + 1) * GDN_DK] = seg * (lax.rsqrt(jnp.sum(seg * seg, -1, keepdims=True) + EPS) * sc)


def _gdn_conv(qkvz, conv_w, dims, n_ctx, rows, tm, tc):
    d, seq, batch = dims
    chans = conv_w.shape[1]
    n_norm = 2 * GDN_K_HEADS * GDN_DK // tc
    rb = tm // SUBLANES
    last_blk = rows // SUBLANES - 1
    return pl.pallas_call(
        functools.partial(_gdn_conv_kernel, tm=tm, seq=seq, batch=batch, n_ctx=n_ctx, n_norm=n_norm),
        out_shape=jax.ShapeDtypeStruct((rows, chans), F32),
        grid=(rows // tm, chans // tc),
        in_specs=[pl.BlockSpec((SUBLANES, tc), lambda i, j: (jnp.maximum(i * rb - 1, 0), j)),
                  pl.BlockSpec((tm, tc), lambda i, j: (i, j)),
                  pl.BlockSpec((SUBLANES, tc), lambda i, j: (jnp.minimum((i + 1) * rb, last_blk), j)),
                  pl.BlockSpec((GDN_CONV, tc), lambda i, j: (0, j))],
        out_specs=pl.BlockSpec((tm, tc), lambda i, j: (i, j)),
        scratch_shapes=[pltpu.VMEM((tm + 2 * SUBLANES, tc), F32)],
        compiler_params=_cparams(("parallel", "arbitrary")),
        name="gdn_conv",
    )(qkvz, qkvz, qkvz, conv_w)


def _gdn_gate_kernel(x_ref, a_ref, dt_ref, o_ref):
    x = x_ref[...]
    lane = lax.broadcasted_iota(jnp.int32, x.shape, 1)
    g = -jnp.exp(a_ref[...]) * _softplus(x + dt_ref[...])
    o_ref[...] = jnp.where(lane % (2 * GDN_V_HEADS) >= GDN_V_HEADS, g, _sigmoid(x))


def _gdn_gates(gates, a_log, dt_bias, rows, tm):
    zeros = jnp.zeros_like(a_log)
    a_row = jnp.stack([zeros, a_log], 1).reshape(1, LANES)
    dt_row = jnp.stack([zeros, dt_bias], 1).reshape(1, LANES)
    return pl.pallas_call(
        _gdn_gate_kernel,
        out_shape=jax.ShapeDtypeStruct((rows, LANES), F32),
        grid=(rows // tm,),
        in_specs=[pl.BlockSpec((tm, LANES), lambda i: (i, 0)),
                  pl.BlockSpec((1, LANES), lambda i: (0, 0)),
                  pl.BlockSpec((1, LANES), lambda i: (0, 0))],
        out_specs=pl.BlockSpec((tm, LANES), lambda i: (i, 0)),
        compiler_params=_cparams(("parallel",)),
        name="gdn_gates",
    )(gates, a_row, dt_row)


def _chunk_masks(d):
    row = lax.broadcasted_iota(jnp.int32, (CHUNK, 2 * CHUNK), 0)
    col = lax.broadcasted_iota(jnp.int32, (CHUNK, 2 * CHUNK), 1)
    rel = (row - col % CHUNK) * (1 - 2 * d)
    return rel >= 0, rel > 0, col < CHUNK, row == col - CHUNK


def _cumsum_tables(x, d):
    r2 = lax.broadcasted_iota(jnp.int32, (2 * CHUNK, CHUNK), 0) % CHUNK
    c2 = lax.broadcasted_iota(jnp.int32, (2 * CHUNK, CHUNK), 1)
    incl2 = jnp.where((r2 - c2) * (1 - 2 * d) >= 0, 1.0, 0.0).astype(BF16)
    hi = x.astype(BF16)
    rest = x - hi.astype(F32)
    mid = rest.astype(BF16)
    lo = (rest - mid.astype(F32)).astype(BF16)
    cs = _dot(incl2, jnp.concatenate([hi, mid, lo], axis=1))
    cs2 = cs[:, :LANES] + cs[:, LANES:2 * LANES] + cs[:, 2 * LANES:]
    return cs2[:CHUNK], cs2.T


def _split_bf16(x):
    hi = x.astype(BF16)
    return hi, (x - hi.astype(F32)).astype(BF16)


def _block_diag_lhs(x, keep, top):
    return jnp.concatenate([jnp.where(keep & top, x, 0.0), jnp.where(keep & ~top, x, 0.0)], axis=1)


def _gdn_scan_kernel(qf_ref, kf_ref, vf_ref, bf_ref, gf_ref, qb_ref, kb_ref, vb_ref, bb_ref, gb_ref,
                     of_ref, ob_ref, s_ref):
    @pl.when(pl.program_id(2) == 0)
    def _():
        s_ref[...] = jnp.zeros_like(s_ref)

    rep = GDN_V_HEADS // GDN_K_HEADS
    n_heads = GDN_GROUP * rep
    streams = ((qf_ref, kf_ref, vf_ref, bf_ref, gf_ref, of_ref), (qb_ref, kb_ref, vb_ref, bb_ref, gb_ref, ob_ref))
    row4 = lax.broadcasted_iota(jnp.int32, (GDN_CHAIN * CHUNK, 2 * CHUNK), 0)
    col4 = lax.broadcasted_iota(jnp.int32, (GDN_CHAIN * CHUNK, 2 * CHUNK), 1)
    p_blk = (row4 // CHUNK) % 2 == col4 // CHUNK
    top = row4 < 2 * CHUNK

    heads = []
    chains = []
    for d, (q_ref, k_ref, v_ref, beta_ref, g_ref, o_ref) in enumerate(streams):
        incl, strict, left, eye_right = _chunk_masks(d)
        eye_left = jnp.logical_and(incl, jnp.logical_and(~strict, left))
        g_all = g_ref[...]
        gc, gct = _cumsum_tables(g_all, d)
        g_tot = jnp.sum(g_all, axis=0, keepdims=True)
        beta_all = beta_ref[...]
        for kh in range(GDN_GROUP):
            ksl = slice(kh * GDN_DK, (kh + 1) * GDN_DK)
            kk_b = k_ref[:, ksl].astype(BF16)
            gram = _dot_nt(jnp.concatenate([kk_b, q_ref[:, ksl].astype(BF16)], axis=0),
                           jnp.concatenate([kk_b, kk_b], axis=0))
            for r in range(rep):
                h = kh * rep + r
                beta = beta_all[:, h:h + 1]
                gcol = gc[:, h:h + 1]
                decay = jnp.exp(jnp.where(incl, gcol - gct[h:h + 1, :], -jnp.inf))
                n_mat = -jnp.where(strict, beta * gram[:CHUNK] * decay, 0.0)
                if h % 2 == 0:
                    ps = jnp.where(left, n_mat, jnp.where(eye_right, 1.0, 0.0))
                else:
                    ps = jnp.where(left, jnp.where(eye_left, 1.0, 0.0), n_mat)
                heads.append(dict(d=d, h=h, ksl=ksl, k_ref=k_ref, q_ref=q_ref, v_ref=v_ref, o_ref=o_ref,
                                  beta=beta, gcol=gcol, tail=g_tot[:, h:h + 1], ps=ps,
                                  attn=gram[CHUNK:] * decay))
        for c in range(n_heads // GDN_CHAIN):
            chains.append([hd for hd in heads if hd["d"] == d][c * GDN_CHAIN:(c + 1) * GDN_CHAIN])
    stacks = [jnp.concatenate([hd["ps"] for hd in ch], axis=0) for ch in chains]

    for _ in range(int(math.log2(CHUNK))):
        for ci in range(len(chains)):
            ps = stacks[ci]
            lhs_hi, lhs_lo = _split_bf16(_block_diag_lhs(ps, p_blk, top))
            ps_hi, ps_lo = _split_bf16(ps)
            wide = _dot(lhs_hi, jnp.concatenate([ps_hi, ps_lo], axis=1))
            stacks[ci] = (wide[:, :2 * CHUNK] + wide[:, 2 * CHUNK:] + _dot(lhs_lo, ps_hi)
                          + jnp.where(p_blk, 0.0, ps))

    for ci, ch in enumerate(chains):
        t_lhs = _block_diag_lhs(stacks[ci], ~p_blk, top).astype(BF16)
        rhs = []
        for j in (1, 0, 3, 2):
            hd = ch[j]
            kb = hd["k_ref"][:, hd["ksl"]] * hd["beta"]
            vsl = slice(hd["h"] * GDN_DV, (hd["h"] + 1) * GDN_DV)
            rhs.append(jnp.concatenate([(hd["v_ref"][:, vsl] * hd["beta"]).astype(BF16),
                                        (kb * jnp.exp(hd["gcol"])).astype(BF16)], axis=1))
        w = _dot(t_lhs, jnp.concatenate(rhs, axis=0))
        for j, hd in enumerate(ch):
            hd["value"] = w[j * CHUNK:(j + 1) * CHUNK, :GDN_DV]
            hd["k_cum"] = w[j * CHUNK:(j + 1) * CHUNK, GDN_DV:]

    for hd in heads:
        state = s_ref[hd["d"] * n_heads + hd["h"]]
        q_dec = hd["q_ref"][:, hd["ksl"]] * jnp.exp(hd["gcol"])
        ks_qs = _dot(jnp.concatenate([hd["k_cum"].astype(BF16), q_dec.astype(BF16)], axis=0), state.astype(BF16))
        hd["state"] = state
        hd["v_new"] = (hd["value"] - ks_qs[:CHUNK]).astype(BF16)
        hd["qs"] = ks_qs[CHUNK:]
    for ch in chains:
        a_lhs = _block_diag_lhs(jnp.concatenate([hd["attn"] for hd in ch], axis=0), p_blk, top).astype(BF16)
        o_attn = _dot(a_lhs, jnp.concatenate([hd["v_new"] for hd in ch], axis=0))
        for j, hd in enumerate(ch):
            hd["o_ref"][:, hd["h"] * GDN_DV:(hd["h"] + 1) * GDN_DV] = hd["qs"] + o_attn[j * CHUNK:(j + 1) * CHUNK]
    for hd in heads:
        k_tail = hd["k_ref"][:, hd["ksl"]] * jnp.exp(hd["tail"] - hd["gcol"])
        s_ref[hd["d"] * n_heads + hd["h"]] = (hd["state"] * jnp.exp(hd["tail"])
                                              + _dot_tn(k_tail.astype(BF16), hd["v_new"]))


def _scan_row_block(d, b, s, seq, batch, n_ctx):
    nc = n_ctx // CHUNK
    nl = seq // CHUNK
    ctx_pos = jnp.where(d == 0, s, nc - 1 - s)
    lat_pos = jnp.where(d == 0, s - nc, nl - 1 - (s - nc))
    return jnp.where(s < nc, (batch * seq + b * n_ctx) // CHUNK + ctx_pos, b * nl + lat_pos)


def _group_gates(x, n_heads, group, rows):
    ng = n_heads // group
    x = x.reshape(rows, 2, 2, ng, group).transpose(2, 1, 3, 0, 4)
    x = jnp.pad(x, ((0, 0), (0, 0), (0, 0), (0, 0), (0, LANES - group)))
    return x[0], x[1]


def _gdn_scan(act, beta, g, dims, n_ctx, rows):
    d_model, seq, batch = dims
    rep = GDN_V_HEADS // GDN_K_HEADS
    ng = GDN_K_HEADS // GDN_GROUP
    gw = GDN_GROUP * GDN_DK
    vw = GDN_GROUP * rep * GDN_DV
    k_blk0 = GDN_K_HEADS * GDN_DK // gw
    v_blk0 = 2 * GDN_K_HEADS * GDN_DK // vw
    steps = (n_ctx + seq) // CHUNK
    rb = functools.partial(_scan_row_block, seq=seq, batch=batch, n_ctx=n_ctx)
    in_specs = []
    for d in range(2):
        in_specs += [pl.BlockSpec((CHUNK, gw), lambda b, hg, s, d=d: (rb(d, b, s), hg)),
                     pl.BlockSpec((CHUNK, gw), lambda b, hg, s, d=d: (rb(d, b, s), k_blk0 + hg)),
                     pl.BlockSpec((CHUNK, vw), lambda b, hg, s, d=d: (rb(d, b, s), v_blk0 + hg)),
                     pl.BlockSpec((None, None, CHUNK, LANES), lambda b, hg, s, d=d: (d, hg, rb(d, b, s), 0)),
                     pl.BlockSpec((None, None, CHUNK, LANES), lambda b, hg, s, d=d: (d, hg, rb(d, b, s), 0))]
    out = jax.ShapeDtypeStruct((rows, GDN_V_HEADS * GDN_DV), F32)
    return pl.pallas_call(
        _gdn_scan_kernel,
        out_shape=(out, out),
        grid=(batch, ng, steps),
        in_specs=in_specs,
        out_specs=tuple(pl.BlockSpec((CHUNK, vw), lambda b, hg, s, d=d: (rb(d, b, s), hg)) for d in range(2)),
        scratch_shapes=[pltpu.VMEM((2 * GDN_GROUP * rep, GDN_DK, GDN_DV), F32)],
        compiler_params=_cparams(("parallel", "parallel", "arbitrary")),
        name="gdn_scan",
    )(act, act, act, beta, g, act, act, act, beta, g)


def _gdn_out_kernel(of_ref, ob_ref, z_ref, res_ref, gn_ref, gate_ref, w_ref, out_ref, acc_ref):
    k = pl.program_id(1)

    @pl.when(k == 0)
    def _():
        acc_ref[...] = jnp.zeros_like(acc_ref)

    o = of_ref[...] + ob_ref[...]
    z = z_ref[...]
    gn = gn_ref[...]
    parts = []
    for h in range(o.shape[1] // GDN_DV):
        seg = o[:, h * GDN_DV:(h + 1) * GDN_DV]
        zz = z[:, h * GDN_DV:(h + 1) * GDN_DV]
        seg = seg * lax.rsqrt(jnp.mean(seg * seg, -1, keepdims=True) + EPS) * gn
        parts.append((seg * (zz * _sigmoid(zz))).astype(BF16))
    acc_ref[...] += _dot(jnp.concatenate(parts, axis=1), w_ref[...])

    @pl.when(k == pl.num_programs(1) - 1)
    def _():
        out_ref[...] = res_ref[...] + gate_ref[0] * acc_ref[...]


def _gdn_out(o, qkvz, res, out_norm, mods, w_o, dims, rows, tm, tkk):
    d = dims[0]
    width = o[0].shape[1]
    z_blk0 = (qkvz.shape[1] - width) // tkk
    return pl.pallas_call(
        _gdn_out_kernel,
        out_shape=jax.ShapeDtypeStruct((rows, d), F32),
        grid=(rows // tm, width // tkk),
        in_specs=[pl.BlockSpec((tm, tkk), lambda i, k: (i, k)),
                  pl.BlockSpec((tm, tkk), lambda i, k: (i, k)),
                  pl.BlockSpec((tm, tkk), lambda i, k: (i, z_blk0 + k)),
                  pl.BlockSpec((tm, d), lambda i, k: (i, 0)),
                  pl.BlockSpec((1, GDN_DV), lambda i, k: (0, 0)),
                  _mod_spec(dims, tm, 2, 2),
                  pl.BlockSpec((tkk, d), lambda i, k: (k, 0))],
        out_specs=pl.BlockSpec((tm, d), lambda i, k: (i, 0)),
        scratch_shapes=[pltpu.VMEM((tm, d), F32)],
        compiler_params=_cparams(("parallel", "arbitrary")),
        name="gdn_out",
    )(o[0], o[1], qkvz, res, out_norm.reshape(1, GDN_DV), mods, w_o)


def _gdn_layer(hs, mods, w, dims, n_ctx):
    d, seq, batch = dims
    rows = hs.shape[0]
    norm1, w_in, conv_w, a_log, dt_bias, out_norm, w_o = w
    main_w = 2 * GDN_K_HEADS * GDN_DK + 2 * GDN_V_HEADS * GDN_DV
    qkvz = _nm_matmul(hs, norm1, mods, (0, 1), w_in[:, :main_w].astype(BF16), dims, rows, 512, 2048, name="gdn_w_in")
    gates = _nm_matmul(hs, norm1, mods, (0, 1), w_in[:, main_w:].astype(BF16), dims, rows, 512, LANES,
                       name="gdn_w_gates")
    act = _gdn_conv(qkvz, conv_w, dims, n_ctx, rows, 256, 1024)
    bg = _gdn_gates(gates, a_log, dt_bias, rows, 512)
    beta, g = _group_gates(bg, GDN_V_HEADS, GDN_GROUP * GDN_V_HEADS // GDN_K_HEADS, rows)
    o = _gdn_scan(act, beta, g, dims, n_ctx, rows)
    return _gdn_out(o, qkvz, hs, out_norm, mods, w_o.astype(BF16), dims, rows, 512, 1024)


def _ml_gate_kernel(x_ref, b_ref, o_ref):
    x = x_ref[...] + b_ref[...]
    lane = lax.broadcasted_iota(jnp.int32, x.shape, 1)
    o_ref[...] = jnp.where(lane % (2 * ML_HEADS) >= ML_HEADS, _log_sigmoid(x), x)


def _ml_gates(gates, gate_b, rows, tm):
    b_row = jnp.pad(gate_b.reshape(1, -1), ((0, 0), (0, LANES - 4 * ML_HEADS)))
    return pl.pallas_call(
        _ml_gate_kernel,
        out_shape=jax.ShapeDtypeStruct((rows, LANES), F32),
        grid=(rows // tm,),
        in_specs=[pl.BlockSpec((tm, LANES), lambda i: (i, 0)),
                  pl.BlockSpec((1, LANES), lambda i: (0, 0))],
        out_specs=pl.BlockSpec((tm, LANES), lambda i: (i, 0)),
        compiler_params=_cparams(("parallel",)),
        name="mlstm_gates",
    )(gates, b_row)


def _ml_scan_kernel(qf_ref, kf_ref, vf_ref, if_ref, ff_ref, qb_ref, kb_ref, vb_ref, ib_ref, fb_ref,
                    of_ref, ob_ref, c_ref, n_ref, m_ref):
    @pl.when(pl.program_id(2) == 0)
    def _():
        c_ref[...] = jnp.zeros_like(c_ref)
        n_ref[...] = jnp.zeros_like(n_ref)
        m_ref[...] = jnp.zeros_like(m_ref)

    streams = ((qf_ref, kf_ref, vf_ref, if_ref, ff_ref, of_ref), (qb_ref, kb_ref, vb_ref, ib_ref, fb_ref, ob_ref))
    row = lax.broadcasted_iota(jnp.int32, (CHUNK, CHUNK), 0)
    col = lax.broadcasted_iota(jnp.int32, (CHUNK, CHUNK), 1)

    heads = []
    for d, (q_ref, k_ref, v_ref, li_ref, lf_ref, o_ref) in enumerate(streams):
        incl = (row - col) * (1 - 2 * d) >= 0
        lf = lf_ref[...]
        li = li_ref[...]
        bc, bct = _cumsum_tables(lf, d)
        b_tot = jnp.sum(lf, axis=0, keepdims=True)
        lit = jnp.concatenate([li, li], axis=0).T
        for h in range(ML_GROUP):
            qh = q_ref[:, h * ML_DQK:(h + 1) * ML_DQK]
            kh = k_ref[:, h * ML_DQK:(h + 1) * ML_DQK] * (ML_DQK ** -0.5)
            bcol = bc[:, h:h + 1]
            b_last = b_tot[:, h:h + 1]
            d_mat = jnp.where(incl, bcol - bct[h:h + 1, :CHUNK] + lit[h:h + 1, :CHUNK], -jnp.inf)
            w_end = b_last - bcol + li[:, h:h + 1]
            heads.append(dict(idx=d * ML_GROUP + h, h=h, q_ref=q_ref, v_ref=v_ref, o_ref=o_ref, kh=kh,
                              bcol=bcol, b_last=b_last, d_mat=d_mat,
                              d_max=jnp.max(d_mat, axis=-1, keepdims=True),
                              qk=_dot_nt(qh.astype(BF16), kh.astype(BF16)),
                              w_end=w_end, w_end_max=jnp.max(w_end, axis=0, keepdims=True)))

    for hd in heads:
        hd["m"] = m_ref[hd["idx"]][0:1, 0:1]
        hd["c"] = c_ref[hd["idx"]]
        hd["nv"] = n_ref[hd["idx"]]
        hd["v"] = hd["v_ref"][:, hd["h"] * ML_DV:(hd["h"] + 1) * ML_DV].astype(BF16)
        inter = hd["bcol"] + hd["m"]
        hd["mt"] = jnp.maximum(hd["d_max"], inter)
        hd["p"] = jnp.exp(hd["d_mat"] - hd["mt"]) * hd["qk"]
        hd["a_in"] = jnp.exp(inter - hd["mt"])
    for hd in heads:
        qh = hd["q_ref"][:, hd["h"] * ML_DQK:(hd["h"] + 1) * ML_DQK]
        num = _dot(hd["p"].astype(BF16), hd["v"]) + hd["a_in"] * _dot(qh.astype(BF16), hd["c"].astype(BF16))
        den = (jnp.sum(hd["p"], axis=-1, keepdims=True)
               + hd["a_in"] * jnp.sum(qh * hd["nv"], axis=-1, keepdims=True))
        hd["o_ref"][:, hd["h"] * ML_DV:(hd["h"] + 1) * ML_DV] = num / jnp.maximum(jnp.abs(den), jnp.exp(-hd["mt"]))
    for hd in heads:
        m_new = jnp.maximum(hd["b_last"] + hd["m"], hd["w_end_max"])
        carry_decay = jnp.exp(hd["b_last"] + hd["m"] - m_new)
        kw = hd["kh"] * jnp.exp(hd["w_end"] - m_new)
        c_ref[hd["idx"]] = carry_decay * hd["c"] + _dot_tn(kw.astype(BF16), hd["v"])
        n_ref[hd["idx"]] = carry_decay * hd["nv"] + jnp.sum(kw, axis=0, keepdims=True)
        m_ref[hd["idx"]] = jnp.broadcast_to(m_new, m_ref.shape[1:])


def _ml_scan(qkvo, li, lf, dims, n_ctx, rows):
    d_model, seq, batch = dims
    ng = ML_HEADS // ML_GROUP
    qw = ML_GROUP * ML_DQK
    vw = ML_GROUP * ML_DV
    k_blk0 = ML_HEADS * ML_DQK // qw
    v_blk0 = 2 * ML_HEADS * ML_DQK // vw
    steps = (n_ctx + seq) // CHUNK
    rb = functools.partial(_scan_row_block, seq=seq, batch=batch, n_ctx=n_ctx)
    in_specs = []
    for d in range(2):
        in_specs += [pl.BlockSpec((CHUNK, qw), lambda b, hg, s, d=d: (rb(d, b, s), hg)),
                     pl.BlockSpec((CHUNK, qw), lambda b, hg, s, d=d: (rb(d, b, s), k_blk0 + hg)),
                     pl.BlockSpec((CHUNK, vw), lambda b, hg, s, d=d: (rb(d, b, s), v_blk0 + hg)),
                     pl.BlockSpec((None, None, CHUNK, LANES), lambda b, hg, s, d=d: (d, hg, rb(d, b, s), 0)),
                     pl.BlockSpec((None, None, CHUNK, LANES), lambda b, hg, s, d=d: (d, hg, rb(d, b, s), 0))]
    out = jax.ShapeDtypeStruct((rows, ML_HEADS * ML_DV), F32)
    return pl.pallas_call(
        _ml_scan_kernel,
        out_shape=(out, out),
        grid=(batch, ng, steps),
        in_specs=in_specs,
        out_specs=tuple(pl.BlockSpec((CHUNK, vw), lambda b, hg, s, d=d: (rb(d, b, s), hg)) for d in range(2)),
        scratch_shapes=[pltpu.VMEM((2 * ML_GROUP, ML_DQK, ML_DV), F32),
                        pltpu.VMEM((2 * ML_GROUP, 1, ML_DQK), F32),
                        pltpu.VMEM((2 * ML_GROUP, SUBLANES, LANES), F32)],
        compiler_params=_cparams(("parallel", "parallel", "arbitrary")),
        name="mlstm_scan",
    )(qkvo, qkvo, qkvo, li, lf, qkvo, qkvo, qkvo, li, lf)


def _ml_out_kernel(hf_ref, hb_ref, og_ref, res_ref, gn_ref, gate_ref, w_ref, out_ref):
    hs = hf_ref[...] + hb_ref[...]
    og = og_ref[...]
    gn = gn_ref[...]
    parts = []
    for h in range(ML_HEADS):
        sl = slice(h * ML_DV, (h + 1) * ML_DV)
        seg = hs[:, sl]
        seg = seg * lax.rsqrt(jnp.mean(seg * seg, -1, keepdims=True) + EPS) * gn[:, sl]
        parts.append((seg * _sigmoid(og[:, sl])).astype(BF16))
    out_ref[...] = res_ref[...] + gate_ref[0] * _dot(jnp.concatenate(parts, axis=1), w_ref[...])


def _ml_out(hh, qkvo, res, out_norm, mods, w_o, dims, rows, tm):
    d = dims[0]
    width = hh[0].shape[1]
    og_blk = (qkvo.shape[1] - width) // width
    return pl.pallas_call(
        _ml_out_kernel,
        out_shape=jax.ShapeDtypeStruct((rows, d), F32),
        grid=(rows // tm,),
        in_specs=[pl.BlockSpec((tm, width), lambda i: (i, 0)),
                  pl.BlockSpec((tm, width), lambda i: (i, 0)),
                  pl.BlockSpec((tm, width), lambda i: (i, og_blk)),
                  pl.BlockSpec((tm, d), lambda i: (i, 0)),
                  pl.BlockSpec((1, width), lambda i: (0, 0)),
                  _mod_spec(dims, tm, 2, 1),
                  pl.BlockSpec((width, d), lambda i: (0, 0))],
        out_specs=pl.BlockSpec((tm, d), lambda i: (i, 0)),
        compiler_params=_cparams(("parallel",)),
        name="mlstm_out",
    )(hh[0], hh[1], qkvo, res, out_norm.reshape(1, width), mods, w_o)


def _mlstm_layer(hs, mods, w, dims, n_ctx):
    d, seq, batch = dims
    rows = hs.shape[0]
    norm1, w_in, gate_b, out_norm, w_o = w
    main_w = 2 * ML_HEADS * ML_DQK + 2 * ML_HEADS * ML_DV
    qkvo = _nm_matmul(hs, norm1, mods, (0, 1), w_in[:, :main_w].astype(BF16), dims, rows, 512, 2048,
                      name="mlstm_w_in")
    w_g = jnp.pad(w_in[:, main_w:], ((0, 0), (0, LANES - 4 * ML_HEADS))).astype(BF16)
    gates = _nm_matmul(hs, norm1, mods, (0, 1), w_g, dims, rows, 512, LANES, name="mlstm_w_gates")
    lg = _ml_gates(gates, gate_b, rows, 512)
    li, lf = _group_gates(lg[:, :4 * ML_HEADS], ML_HEADS, ML_GROUP, rows)
    hh = _ml_scan(qkvo, li, lf, dims, n_ctx, rows)
    return _ml_out(hh, qkvo, hs, out_norm, mods, w_o.astype(BF16), dims, rows, 512)


def _final_norm_kernel(x_ref, g_ref, o_ref):
    x = x_ref[...]
    o_ref[...] = x * lax.rsqrt(jnp.mean(x * x, -1, keepdims=True) + EPS) * g_ref[...]


def _final_norm(x, g, rows, tm):
    d = x.shape[1]
    return pl.pallas_call(
        _final_norm_kernel,
        out_shape=jax.ShapeDtypeStruct((rows, d), F32),
        grid=(rows // tm,),
        in_specs=[pl.BlockSpec((tm, d), lambda i: (i, 0)),
                  pl.BlockSpec((1, d), lambda i: (0, 0))],
        out_specs=pl.BlockSpec((tm, d), lambda i: (i, 0)),
        compiler_params=_cparams(("parallel",)),
        name="final_norm",
    )(x, g.reshape(1, d))


def kernel(x, c, ctx, c_ctx, l0_ada_w, l0_ada_b, l0_norm1, l0_mla_w_in, l0_mla_q_norm, l0_mla_kv_norm, l0_mla_w_q_up, l0_mla_w_kv_up, l0_mla_w_o, l0_norm2, l0_mlp_w1, l0_mlp_w2, l1_ada_w, l1_ada_b, l1_norm1, l1_gdn_w_in, l1_gdn_conv_w, l1_gdn_a_log, l1_gdn_dt_bias, l1_gdn_out_norm, l1_gdn_w_o, l1_norm2, l1_mlp_w1, l1_mlp_w2, l2_ada_w, l2_ada_b, l2_norm1, l2_mlstm_w_in, l2_mlstm_gate_b, l2_mlstm_out_norm, l2_mlstm_w_o, l2_norm2, l2_mlp_w1, l2_mlp_w2, l3_ada_w, l3_ada_b, l3_norm1, l3_mla_w_in, l3_mla_q_norm, l3_mla_kv_norm, l3_mla_w_q_up, l3_mla_w_kv_up, l3_mla_w_o, l3_norm2, l3_mlp_w1, l3_mlp_w2, final_norm):
    batch, seq, d = x.shape
    n_ctx = ctx.shape[1]
    dims = (d, seq, batch)
    lat_rows = batch * seq
    rows = lat_rows + batch * n_ctx
    layers = (
        ("mla", (l0_ada_w, l0_ada_b, l0_norm2, l0_mlp_w1, l0_mlp_w2),
         (l0_norm1, l0_mla_w_in, l0_mla_q_norm, l0_mla_kv_norm, l0_mla_w_q_up, l0_mla_w_kv_up, l0_mla_w_o)),
        ("gdn", (l1_ada_w, l1_ada_b, l1_norm2, l1_mlp_w1, l1_mlp_w2),
         (l1_norm1, l1_gdn_w_in, l1_gdn_conv_w, l1_gdn_a_log, l1_gdn_dt_bias, l1_gdn_out_norm, l1_gdn_w_o)),
        ("mlstm", (l2_ada_w, l2_ada_b, l2_norm2, l2_mlp_w1, l2_mlp_w2),
         (l2_norm1, l2_mlstm_w_in, l2_mlstm_gate_b, l2_mlstm_out_norm, l2_mlstm_w_o)),
        ("mla", (l3_ada_w, l3_ada_b, l3_norm2, l3_mlp_w1, l3_mlp_w2),
         (l3_norm1, l3_mla_w_in, l3_mla_q_norm, l3_mla_kv_norm, l3_mla_w_q_up, l3_mla_w_kv_up, l3_mla_w_o)),
    )
    cond = jnp.concatenate([c, c_ctx[None, :], jnp.zeros((SUBLANES - batch - 1, d), F32)], 0)
    cos, sin = _rope_tables(seq)
    ctx_pad = ((0, batch * n_ctx), (0, 0))
    rope = (jnp.pad(jnp.tile(cos, (batch, 1)), ctx_pad, constant_values=1.0),
            jnp.pad(jnp.tile(sin, (batch, 1)), ctx_pad))
    hs = jnp.concatenate([x.reshape(lat_rows, d), ctx.reshape(batch * n_ctx, d)], 0)
    for li, (kind, (ada_w, ada_b, norm2, mlp_w1, mlp_w2), mixer_w) in enumerate(layers):
        ctx_out = li < len(layers) - 1
        mods = _ada(cond, ada_w, ada_b)[:batch + 1].reshape(batch + 1, 1, N_MOD * d)
        if kind == "mla":
            hs = _mla_layer(hs, mods, mixer_w, dims, n_ctx, ctx_out, rope)
        elif kind == "gdn":
            hs = _gdn_layer(hs, mods, mixer_w, dims, n_ctx)
        else:
            hs = _mlstm_layer(hs, mods, mixer_w, dims, n_ctx)
        hs = _mlp(hs, norm2, mods, mlp_w1.astype(BF16), mlp_w2.astype(BF16), dims, hs.shape[0], 512, 1024)
    return _final_norm(hs, final_norm, lat_rows, 512).reshape(batch, seq, d)
```

```python
import functools
import math

import jax
import jax.numpy as jnp
from jax import lax
from jax.experimental import pallas as pl
from jax.experimental.pallas import tpu as pltpu

F32 = jnp.float32
BF16 = jnp.bfloat16

EPS = 1e-6
ROPE_THETA = 10000.0
GRID_W = 64
N_MOD = 6

MLA_HEADS = 16
MLA_Q_RANK = 768
MLA_KV_RANK = 512
MLA_NOPE = 128
MLA_ROPE = 64
MLA_V = 128
MLA_QK_PAD = 256
ATTN_UNROLL = 16

GDN_K_HEADS = 16
GDN_V_HEADS = 32
GDN_DK = 128
GDN_DV = 128
GDN_CONV = 5
GDN_GROUP = 4
GDN_CHAIN = 4

ML_HEADS = 8
ML_DQK = 128
ML_DV = 256
ML_GROUP = 8

CHUNK = 64
LANES = 128
SUBLANES = 8

VMEM_LIMIT = 52 * 1024 * 1024


def _cparams(sem):
    return pltpu.CompilerParams(dimension_semantics=sem, vmem_limit_bytes=VMEM_LIMIT)


def _dot(a, b):
    return jnp.dot(a, b, preferred_element_type=F32)


def _dot_nt(a, b):
    return lax.dot_general(a, b, (((1,), (1,)), ((), ())), preferred_element_type=F32)


def _dot_tn(a, b):
    return lax.dot_general(a, b, (((0,), (0,)), ((), ())), preferred_element_type=F32)


def _normmod(x, g, shift, scale):
    y = x * lax.rsqrt(jnp.mean(x * x, axis=-1, keepdims=True) + EPS) * g
    return y * (1.0 + scale) + shift


def _sigmoid(x):
    return 1.0 / (1.0 + jnp.exp(-x))


def _softplus(x):
    return jnp.maximum(x, 0.0) + jnp.log(1.0 + jnp.exp(-jnp.abs(x)))


def _log_sigmoid(x):
    return -_softplus(-x)


def _ada_kernel(c_ref, w_ref, b_ref, o_ref):
    c = c_ref[...]
    s = (c * _sigmoid(c)).astype(BF16)
    o_ref[...] = _dot(s, w_ref[...].astype(BF16)) + b_ref[...]


def _ada(cond, w, b):
    rows, d = cond.shape
    n = w.shape[1]
    tn = 1024
    return pl.pallas_call(
        _ada_kernel,
        out_shape=jax.ShapeDtypeStruct((rows, n), F32),
        grid=(n // tn,),
        in_specs=[pl.BlockSpec((rows, d), lambda j: (0, 0)),
                  pl.BlockSpec((d, tn), lambda j: (0, j)),
                  pl.BlockSpec((1, tn), lambda j: (0, j))],
        out_specs=pl.BlockSpec((rows, tn), lambda j: (0, j)),
        compiler_params=_cparams(("arbitrary",)),
        name="ada_mod",
    )(cond, w, b.reshape(1, n))


def _nm_matmul_kernel(x_ref, g_ref, sh_ref, sc_ref, w_ref, o_ref, u_ref):
    @pl.when(pl.program_id(1) == 0)
    def _():
        u_ref[...] = _normmod(x_ref[...], g_ref[...], sh_ref[0], sc_ref[0]).astype(BF16)

    o_ref[...] = _dot(u_ref[...], w_ref[...]).astype(o_ref.dtype)


def _mod_spec(dims, tm, kind, ngrid):
    d, seq, batch = dims
    if ngrid == 1:
        return pl.BlockSpec((1, 1, d), lambda i: (jnp.minimum(i * tm // seq, batch), 0, kind))
    return pl.BlockSpec((1, 1, d), lambda i, j: (jnp.minimum(i * tm // seq, batch), 0, kind))


def _nm_matmul(x, g, mods, kinds, w, dims, rows, tm, tn, out_dtype=F32, name="nm_matmul"):
    d = dims[0]
    n = w.shape[1]
    return pl.pallas_call(
        _nm_matmul_kernel,
        out_shape=jax.ShapeDtypeStruct((rows, n), out_dtype),
        grid=(rows // tm, n // tn),
        in_specs=[pl.BlockSpec((tm, d), lambda i, j: (i, 0)),
                  pl.BlockSpec((1, d), lambda i, j: (0, 0)),
                  _mod_spec(dims, tm, kinds[0], 2),
                  _mod_spec(dims, tm, kinds[1], 2),
                  pl.BlockSpec((d, tn), lambda i, j: (0, j))],
        out_specs=pl.BlockSpec((tm, tn), lambda i, j: (i, j)),
        scratch_shapes=[pltpu.VMEM((tm, d), BF16)],
        compiler_params=_cparams(("parallel", "arbitrary")),
        name=name,
    )(x, g.reshape(1, d), mods, mods, w)


def _mlp_kernel(x_ref, g_ref, sh_ref, sc_ref, gate_ref, w1_ref, w2_ref, o_ref, u_ref, acc_ref):
    k = pl.program_id(1)

    @pl.when(k == 0)
    def _():
        u_ref[...] = _normmod(x_ref[...], g_ref[...], sh_ref[0], sc_ref[0]).astype(BF16)
        acc_ref[...] = jnp.zeros_like(acc_ref)

    h = jnp.maximum(_dot(u_ref[...], w1_ref[...]), 0.0)
    acc_ref[...] += _dot((h * h).astype(BF16), w2_ref[...])

    @pl.when(k == pl.num_programs(1) - 1)
    def _():
        o_ref[...] = x_ref[...] + gate_ref[0] * acc_ref[...]


def _mlp(x, g, mods, w1, w2, dims, rows, tm, tk):
    d = dims[0]
    hidden = w1.shape[1]
    return pl.pallas_call(
        _mlp_kernel,
        out_shape=jax.ShapeDtypeStruct((rows, d), F32),
        grid=(rows // tm, hidden // tk),
        in_specs=[pl.BlockSpec((tm, d), lambda i, k: (i, 0)),
                  pl.BlockSpec((1, d), lambda i, k: (0, 0)),
                  _mod_spec(dims, tm, 3, 2),
                  _mod_spec(dims, tm, 4, 2),
                  _mod_spec(dims, tm, 5, 2),
                  pl.BlockSpec((d, tk), lambda i, k: (0, k)),
                  pl.BlockSpec((tk, d), lambda i, k: (k, 0))],
        out_specs=pl.BlockSpec((tm, d), lambda i, k: (i, 0)),
        scratch_shapes=[pltpu.VMEM((tm, d), BF16), pltpu.VMEM((tm, d), F32)],
        compiler_params=_cparams(("parallel", "arbitrary")),
        name="mlp",
    )(x, g.reshape(1, d), mods, mods, mods, w1, w2)


def _proj_res_kernel(*refs, n_lat_blocks):
    if n_lat_blocks is None:
        a_ref, res_ref, gate_ref, w_ref, o_ref = refs
        o_ref[...] = res_ref[...] + gate_ref[0] * _dot(a_ref[...], w_ref[...])
        return
    a_ref, ac_ref, res_ref, gate_ref, w_ref, o_ref = refs
    is_lat = pl.program_id(0) < n_lat_blocks

    @pl.when(is_lat)
    def _():
        o_ref[...] = res_ref[...] + gate_ref[0] * _dot(a_ref[...], w_ref[...])

    @pl.when(jnp.logical_not(is_lat))
    def _():
        o_ref[...] = res_ref[...] + gate_ref[0] * _dot(ac_ref[...], w_ref[...])


def _proj_res(a, a_ctx, res, mods, w, dims, rows, tm):
    d = dims[0]
    kdim = a.shape[1]
    nl = a.shape[0] // tm
    a_specs = [pl.BlockSpec((tm, kdim), lambda i: (jnp.minimum(i, nl - 1), 0))]
    a_args = [a]
    if a_ctx is not None:
        a_specs.append(pl.BlockSpec((tm, kdim), lambda i: (jnp.maximum(i - nl, 0), 0)))
        a_args.append(a_ctx)
    return pl.pallas_call(
        functools.partial(_proj_res_kernel, n_lat_blocks=None if a_ctx is None else nl),
        out_shape=jax.ShapeDtypeStruct((rows, d), F32),
        grid=(rows // tm,),
        in_specs=a_specs + [pl.BlockSpec((tm, d), lambda i: (i, 0)),
                            _mod_spec(dims, tm, 2, 1),
                            pl.BlockSpec((kdim, d), lambda i: (0, 0))],
        out_specs=pl.BlockSpec((tm, d), lambda i: (i, 0)),
        compiler_params=_cparams(("parallel",)),
        name="proj_res",
    )(*a_args, res, mods, w)


def _rope(x, cos, sin):
    lane = lax.broadcasted_iota(jnp.int32, x.shape, 1)
    partner = jnp.where(lane % 32 < 16, pltpu.roll(x, LANES - 16, 1), pltpu.roll(x, 16, 1))
    return x * cos + partner * sin


def _mla_up_kernel(c_ref, qn_ref, kvn_ref, wqn_ref, wqr_ref, wkn_ref, wv_ref, cos_ref, sin_ref,
                   q_ref, k_ref, v_ref):
    c = c_ref[...]
    cq = c[:, :MLA_Q_RANK]
    ckv = c[:, MLA_Q_RANK:MLA_Q_RANK + MLA_KV_RANK]
    kr = c[:, MLA_Q_RANK + MLA_KV_RANK:]
    cq = (cq * lax.rsqrt(jnp.mean(cq * cq, -1, keepdims=True) + EPS) * qn_ref[...]).astype(BF16)
    ckv = (ckv * lax.rsqrt(jnp.mean(ckv * ckv, -1, keepdims=True) + EPS) * kvn_ref[...]).astype(BF16)
    cos = cos_ref[...]
    sin = sin_ref[...]
    scale = (MLA_NOPE + MLA_ROPE) ** -0.5 * math.log2(math.e)
    qn = _dot(cq, wqn_ref[...]) * scale
    qr = _dot(cq, wqr_ref[...]) * scale
    kn = _dot(ckv, wkn_ref[...])
    v_ref[...] = _dot(ckv, wv_ref[...]).astype(v_ref.dtype)
    kr = _rope(kr, cos, sin).astype(k_ref.dtype)
    for h in range(MLA_HEADS):
        lo = h * MLA_QK_PAD
        sl = slice(h * LANES, (h + 1) * LANES)
        q_ref[:, lo:lo + LANES] = qn[:, sl].astype(q_ref.dtype)
        q_ref[:, lo + LANES:lo + 2 * LANES] = _rope(qr[:, sl], cos, sin).astype(q_ref.dtype)
        k_ref[:, lo:lo + LANES] = kn[:, sl].astype(k_ref.dtype)
        k_ref[:, lo + LANES:lo + 2 * LANES] = kr


def _mla_up(c, q_norm, kv_norm, wqn, wqr, wkn, wv, cos, sin, rows, tm):
    cw = c.shape[1]
    hq = MLA_HEADS * MLA_QK_PAD
    hv = MLA_HEADS * MLA_V
    full = lambda a: pl.BlockSpec(a.shape, lambda i: (0, 0))
    return pl.pallas_call(
        _mla_up_kernel,
        out_shape=(jax.ShapeDtypeStruct((rows, hq), BF16),
                   jax.ShapeDtypeStruct((rows, hq), BF16),
                   jax.ShapeDtypeStruct((rows, hv), BF16)),
        grid=(rows // tm,),
        in_specs=[pl.BlockSpec((tm, cw), lambda i: (i, 0)),
                  full(q_norm), full(kv_norm), full(wqn), full(wqr), full(wkn), full(wv),
                  pl.BlockSpec((tm, LANES), lambda i: (i, 0)),
                  pl.BlockSpec((tm, LANES), lambda i: (i, 0))],
        out_specs=(pl.BlockSpec((tm, hq), lambda i: (i, 0)),
                   pl.BlockSpec((tm, hq), lambda i: (i, 0)),
                   pl.BlockSpec((tm, hv), lambda i: (i, 0))),
        compiler_params=_cparams(("parallel",)),
        name="mla_up",
    )(c, q_norm, kv_norm, wqn, wqr, wkn, wv, cos, sin)


def _attn_kernel(*refs, n_lat, tk):
    if n_lat:
        q_ref, kl_ref, vl_ref, kc_ref, vc_ref, o_ref = refs
    else:
        q_ref, kc_ref, vc_ref, o_ref = refs
    q = q_ref[...]
    tq = q.shape[0]

    def step(k, v, carry):
        m, acc = carry
        s = _dot_nt(q, k)
        m_new = jnp.maximum(m, jnp.max(s, axis=-1, keepdims=True))
        p = jnp.exp2(s - m_new).astype(BF16)
        v_ext = jnp.concatenate([v, jnp.ones_like(v)], axis=1)
        return m_new, jnp.exp2(m - m_new) * acc + _dot(p, v_ext)

    carry = (jnp.full((tq, 1), -jnp.inf, F32), jnp.zeros((tq, 2 * MLA_V), F32))
    if n_lat:
        def body(i, carry):
            off = pl.multiple_of(i * tk, tk)
            return step(kl_ref[pl.ds(off, tk), :], vl_ref[pl.ds(off, tk), :], carry)

        carry = lax.fori_loop(0, n_lat // tk, body, carry, unroll=math.gcd(ATTN_UNROLL, n_lat // tk))
    m, acc = step(kc_ref[...], vc_ref[...], carry)
    o_ref[...] = (acc[:, :MLA_V] / acc[:, MLA_V:]).astype(o_ref.dtype)


def _attention(q, k, v, dims, n_ctx, tq, tk, ctx_queries):
    d, seq, batch = dims
    ctx_blk0 = batch * seq // n_ctx
    kc_spec = pl.BlockSpec((n_ctx, MLA_QK_PAD), lambda b, h, i: (ctx_blk0 + b, h))
    vc_spec = pl.BlockSpec((n_ctx, MLA_V), lambda b, h, i: (ctx_blk0 + b, h))
    if not ctx_queries:
        nq = seq // tq
        return pl.pallas_call(
            functools.partial(_attn_kernel, n_lat=seq, tk=tk),
            out_shape=jax.ShapeDtypeStruct((batch * seq, MLA_HEADS * MLA_V), BF16),
            grid=(batch, MLA_HEADS, nq),
            in_specs=[pl.BlockSpec((tq, MLA_QK_PAD), lambda b, h, i: (b * nq + i, h)),
                      pl.BlockSpec((seq, MLA_QK_PAD), lambda b, h, i: (b, h)),
                      pl.BlockSpec((seq, MLA_V), lambda b, h, i: (b, h)),
                      kc_spec, vc_spec],
            out_specs=pl.BlockSpec((tq, MLA_V), lambda b, h, i: (b * nq + i, h)),
            compiler_params=_cparams(("parallel", "parallel", "arbitrary")),
            name="mla_attention",
        )(q, k, v, k, v)
    return pl.pallas_call(
        functools.partial(_attn_kernel, n_lat=0, tk=tk),
        out_shape=jax.ShapeDtypeStruct((batch * n_ctx, MLA_HEADS * MLA_V), BF16),
        grid=(batch, MLA_HEADS, 1),
        in_specs=[pl.BlockSpec((n_ctx, MLA_QK_PAD), lambda b, h, i: (ctx_blk0 + b, h)), kc_spec, vc_spec],
        out_specs=pl.BlockSpec((n_ctx, MLA_V), lambda b, h, i: (b, h)),
        compiler_params=_cparams(("parallel", "parallel", "arbitrary")),
        name="mla_attention_ctx",
    )(q, k, v)


def _rope_tables(seq):
    t = jnp.arange(seq)
    row = (t // GRID_W).astype(F32)
    col = (t % GRID_W).astype(F32)
    axis_dim = MLA_ROPE // 2
    freqs = jnp.power(ROPE_THETA, -jnp.arange(0, axis_dim, 2, dtype=F32) / axis_dim)
    ar = row[:, None] * freqs[None, :]
    ac = col[:, None] * freqs[None, :]
    cr, sr, cc, sc = jnp.cos(ar), jnp.sin(ar), jnp.cos(ac), jnp.sin(ac)
    pad1 = jnp.ones((seq, LANES - MLA_ROPE), F32)
    pad0 = jnp.zeros((seq, LANES - MLA_ROPE), F32)
    cos = jnp.concatenate([cr, cr, cc, cc, pad1], -1)
    sin = jnp.concatenate([-sr, sr, -sc, sc, pad0], -1)
    return cos, sin


def _mla_layer(hs, mods, w, dims, n_ctx, ctx_out, rope):
    d, seq, batch = dims
    rows = hs.shape[0]
    lat_rows = batch * seq
    norm1, w_in, q_norm, kv_norm, w_q_up, w_kv_up, w_o = w
    c_w = MLA_Q_RANK + MLA_KV_RANK
    w_in_p = jnp.pad(w_in, ((0, 0), (0, LANES - MLA_ROPE))).astype(BF16)
    c = _nm_matmul(hs, norm1, mods, (0, 1), w_in_p, dims, rows, 256, w_in_p.shape[1], name="mla_w_in")
    wq = w_q_up.reshape(MLA_Q_RANK, MLA_HEADS, MLA_NOPE + MLA_ROPE)
    wqn = wq[:, :, :MLA_NOPE].reshape(MLA_Q_RANK, -1).astype(BF16)
    wqr = jnp.pad(wq[:, :, MLA_NOPE:], ((0, 0), (0, 0), (0, LANES - MLA_ROPE))).reshape(MLA_Q_RANK, -1).astype(BF16)
    wkv = w_kv_up.reshape(MLA_KV_RANK, MLA_HEADS, MLA_NOPE + MLA_V)
    wkn = wkv[:, :, :MLA_NOPE].reshape(MLA_KV_RANK, -1).astype(BF16)
    wv = wkv[:, :, MLA_NOPE:].reshape(MLA_KV_RANK, -1).astype(BF16)
    cos, sin = rope
    q, k, v = _mla_up(c, q_norm.reshape(1, -1), kv_norm.reshape(1, -1), wqn, wqr, wkn, wv, cos, sin, rows, 256)
    tq = min(512, seq)
    tk = min(512, seq)
    o = _attention(q, k, v, dims, n_ctx, tq, tk, False)
    o_ctx = _attention(q, k, v, dims, n_ctx, tq, tk, True) if ctx_out else None
    return _proj_res(o, o_ctx, hs, mods, w_o.astype(BF16), dims, rows if ctx_out else lat_rows, 512)


def _seq_edges(i, tm, seq, batch, n_ctx):
    r0 = i * tm
    lat = r0 < batch * seq
    rel = jnp.where(lat, r0 % seq, (r0 - batch * seq) % n_ctx)
    length = jnp.where(lat, seq, n_ctx)
    return rel == 0, rel + tm == length


def _gdn_conv_kernel(prev_ref, cur_ref, next_ref, w_ref, o_ref, ext_ref, *, tm, seq, batch, n_ctx, n_norm):
    i = pl.program_id(0)
    j = pl.program_id(1)
    first, last = _seq_edges(i, tm, seq, batch, n_ctx)
    ext_ref[0:SUBLANES, :] = jnp.where(first, 0.0, prev_ref[...])
    ext_ref[SUBLANES:SUBLANES + tm, :] = cur_ref[...]
    ext_ref[SUBLANES + tm:, :] = jnp.where(last, 0.0, next_ref[...])
    pad = GDN_CONV // 2
    y = None
    for t in range(GDN_CONV):
        term = w_ref[t:t + 1, :] * ext_ref[pl.ds(SUBLANES - pad + t, tm), :]
        y = term if y is None else y + term
    y = y * _sigmoid(y)

    @pl.when(j >= n_norm)
    def _():
        o_ref[...] = y

    @pl.when(j < n_norm)
    def _():
        sc = jnp.where(j < n_norm // 2, GDN_DK ** -0.5, 1.0)
        for h in range(y.shape[1] // GDN_DK):
            seg = y[:, h * GDN_DK:(h + 1) * GDN_DK]
            o_ref[:, h * GDN_DK:(h + 1) * GDN_DK] = seg * (lax.rsqrt(jnp.sum(seg * seg, -1, keepdims=True) + EPS) * sc)


def _gdn_conv(qkvz, conv_w, dims, n_ctx, rows, tm, tc):
    d, seq, batch = dims
    chans = conv_w.shape[1]
    n_norm = 2 * GDN_K_HEADS * GDN_DK // tc
    rb = tm // SUBLANES
    last_blk = rows // SUBLANES - 1
    return pl.pallas_call(
        functools.partial(_gdn_conv_kernel, tm=tm, seq=seq, batch=batch, n_ctx=n_ctx, n_norm=n_norm),
        out_shape=jax.ShapeDtypeStruct((rows, chans), F32),
        grid=(rows // tm, chans // tc),
        in_specs=[pl.BlockSpec((SUBLANES, tc), lambda i, j: (jnp.maximum(i * rb - 1, 0), j)),
                  pl.BlockSpec((tm, tc), lambda i, j: (i, j)),
                  pl.BlockSpec((SUBLANES, tc), lambda i, j: (jnp.minimum((i + 1) * rb, last_blk), j)),
                  pl.BlockSpec((GDN_CONV, tc), lambda i, j: (0, j))],
        out_specs=pl.BlockSpec((tm, tc), lambda i, j: (i, j)),
        scratch_shapes=[pltpu.VMEM((tm + 2 * SUBLANES, tc), F32)],
        compiler_params=_cparams(("parallel", "arbitrary")),
        name="gdn_conv",
    )(qkvz, qkvz, qkvz, conv_w)


def _gdn_gate_kernel(x_ref, a_ref, dt_ref, o_ref):
    x = x_ref[...]
    lane = lax.broadcasted_iota(jnp.int32, x.shape, 1)
    g = -jnp.exp(a_ref[...]) * _softplus(x + dt_ref[...])
    o_ref[...] = jnp.where(lane % (2 * GDN_V_HEADS) >= GDN_V_HEADS, g, _sigmoid(x))


def _gdn_gates(gates, a_log, dt_bias, rows, tm):
    zeros = jnp.zeros_like(a_log)
    a_row = jnp.stack([zeros, a_log], 1).reshape(1, LANES)
    dt_row = jnp.stack([zeros, dt_bias], 1).reshape(1, LANES)
    return pl.pallas_call(
        _gdn_gate_kernel,
        out_shape=jax.ShapeDtypeStruct((rows, LANES), F32),
        grid=(rows // tm,),
        in_specs=[pl.BlockSpec((tm, LANES), lambda i: (i, 0)),
                  pl.BlockSpec((1, LANES), lambda i: (0, 0)),
                  pl.BlockSpec((1, LANES), lambda i: (0, 0))],
        out_specs=pl.BlockSpec((tm, LANES), lambda i: (i, 0)),
        compiler_params=_cparams(("parallel",)),
        name="gdn_gates",
    )(gates, a_row, dt_row)


def _chunk_masks(d):
    row = lax.broadcasted_iota(jnp.int32, (CHUNK, 2 * CHUNK), 0)
    col = lax.broadcasted_iota(jnp.int32, (CHUNK, 2 * CHUNK), 1)
    rel = (row - col % CHUNK) * (1 - 2 * d)
    return rel >= 0, rel > 0, col < CHUNK, row == col - CHUNK


def _cumsum_tables(x, d):
    r2 = lax.broadcasted_iota(jnp.int32, (2 * CHUNK, CHUNK), 0) % CHUNK
    c2 = lax.broadcasted_iota(jnp.int32, (2 * CHUNK, CHUNK), 1)
    incl2 = jnp.where((r2 - c2) * (1 - 2 * d) >= 0, 1.0, 0.0).astype(BF16)
    hi = x.astype(BF16)
    rest = x - hi.astype(F32)
    mid = rest.astype(BF16)
    lo = (rest - mid.astype(F32)).astype(BF16)
    cs = _dot(incl2, jnp.concatenate([hi, mid, lo], axis=1))
    cs2 = cs[:, :LANES] + cs[:, LANES:2 * LANES] + cs[:, 2 * LANES:]
    return cs2[:CHUNK], cs2.T


def _split_bf16(x):
    hi = x.astype(BF16)
    return hi, (x - hi.astype(F32)).astype(BF16)


def _block_diag_lhs(x, keep, top):
    return jnp.concatenate([jnp.where(keep & top, x, 0.0), jnp.where(keep & ~top, x, 0.0)], axis=1)


def _gdn_scan_kernel(qf_ref, kf_ref, vf_ref, bf_ref, gf_ref, qb_ref, kb_ref, vb_ref, bb_ref, gb_ref,
                     of_ref, ob_ref, s_ref):
    @pl.when(pl.program_id(2) == 0)
    def _():
        s_ref[...] = jnp.zeros_like(s_ref)

    rep = GDN_V_HEADS // GDN_K_HEADS
    n_heads = GDN_GROUP * rep
    streams = ((qf_ref, kf_ref, vf_ref, bf_ref, gf_ref, of_ref), (qb_ref, kb_ref, vb_ref, bb_ref, gb_ref, ob_ref))
    row4 = lax.broadcasted_iota(jnp.int32, (GDN_CHAIN * CHUNK, 2 * CHUNK), 0)
    col4 = lax.broadcasted_iota(jnp.int32, (GDN_CHAIN * CHUNK, 2 * CHUNK), 1)
    p_blk = (row4 // CHUNK) % 2 == col4 // CHUNK
    top = row4 < 2 * CHUNK

    heads = []
    chains = []
    for d, (q_ref, k_ref, v_ref, beta_ref, g_ref, o_ref) in enumerate(streams):
        incl, strict, left, eye_right = _chunk_masks(d)
        eye_left = jnp.logical_and(incl, jnp.logical_and(~strict, left))
        g_all = g_ref[...]
        gc, gct = _cumsum_tables(g_all, d)
        g_tot = jnp.sum(g_all, axis=0, keepdims=True)
        beta_all = beta_ref[...]
        for kh in range(GDN_GROUP):
            ksl = slice(kh * GDN_DK, (kh + 1) * GDN_DK)
            kk_b = k_ref[:, ksl].astype(BF16)
            gram = _dot_nt(jnp.concatenate([kk_b, q_ref[:, ksl].astype(BF16)], axis=0),
                           jnp.concatenate([kk_b, kk_b], axis=0))
            for r in range(rep):
                h = kh * rep + r
                beta = beta_all[:, h:h + 1]
                gcol = gc[:, h:h + 1]
                decay = jnp.exp(jnp.where(incl, gcol - gct[h:h + 1, :], -jnp.inf))
                n_mat = -jnp.where(strict, beta * gram[:CHUNK] * decay, 0.0)
                if h % 2 == 0:
                    ps = jnp.where(left, n_mat, jnp.where(eye_right, 1.0, 0.0))
                else:
                    ps = jnp.where(left, jnp.where(eye_left, 1.0, 0.0), n_mat)
                heads.append(dict(d=d, h=h, ksl=ksl, k_ref=k_ref, q_ref=q_ref, v_ref=v_ref, o_ref=o_ref,
                                  beta=beta, gcol=gcol, tail=g_tot[:, h:h + 1], ps=ps,
                                  attn=gram[CHUNK:] * decay))
        for c in range(n_heads // GDN_CHAIN):
            chains.append([hd for hd in heads if hd["d"] == d][c * GDN_CHAIN:(c + 1) * GDN_CHAIN])
    stacks = [jnp.concatenate([hd["ps"] for hd in ch], axis=0) for ch in chains]

    for _ in range(int(math.log2(CHUNK))):
        for ci in range(len(chains)):
            ps = stacks[ci]
            lhs_hi, lhs_lo = _split_bf16(_block_diag_lhs(ps, p_blk, top))
            ps_hi, ps_lo = _split_bf16(ps)
            wide = _dot(lhs_hi, jnp.concatenate([ps_hi, ps_lo], axis=1))
            stacks[ci] = (wide[:, :2 * CHUNK] + wide[:, 2 * CHUNK:] + _dot(lhs_lo, ps_hi)
                          + jnp.where(p_blk, 0.0, ps))

    for ci, ch in enumerate(chains):
        t_lhs = _block_diag_lhs(stacks[ci], ~p_blk, top).astype(BF16)
        rhs = []
        for j in (1, 0, 3, 2):
            hd = ch[j]
            kb = hd["k_ref"][:, hd["ksl"]] * hd["beta"]
            vsl = slice(hd["h"] * GDN_DV, (hd["h"] + 1) * GDN_DV)
            rhs.append(jnp.concatenate([(hd["v_ref"][:, vsl] * hd["beta"]).astype(BF16),
                                        (kb * jnp.exp(hd["gcol"])).astype(BF16)], axis=1))
        w = _dot(t_lhs, jnp.concatenate(rhs, axis=0))
        for j, hd in enumerate(ch):
            hd["value"] = w[j * CHUNK:(j + 1) * CHUNK, :GDN_DV]
            hd["k_cum"] = w[j * CHUNK:(j + 1) * CHUNK, GDN_DV:]

    for hd in heads:
        state = s_ref[hd["d"] * n_heads + hd["h"]]
        q_dec = hd["q_ref"][:, hd["ksl"]] * jnp.exp(hd["gcol"])
        ks_qs = _dot(jnp.concatenate([hd["k_cum"].astype(BF16), q_dec.astype(BF16)], axis=0), state.astype(BF16))
        hd["state"] = state
        hd["v_new"] = (hd["value"] - ks_qs[:CHUNK]).astype(BF16)
        hd["qs"] = ks_qs[CHUNK:]
    for ch in chains:
        a_lhs = _block_diag_lhs(jnp.concatenate([hd["attn"] for hd in ch], axis=0), p_blk, top).astype(BF16)
        o_attn = _dot(a_lhs, jnp.concatenate([hd["v_new"] for hd in ch], axis=0))
        for j, hd in enumerate(ch):
            hd["o_ref"][:, hd["h"] * GDN_DV:(hd["h"] + 1) * GDN_DV] = hd["qs"] + o_attn[j * CHUNK:(j + 1) * CHUNK]
    for hd in heads:
        k_tail = hd["k_ref"][:, hd["ksl"]] * jnp.exp(hd["tail"] - hd["gcol"])
        s_ref[hd["d"] * n_heads + hd["h"]] = (hd["state"] * jnp.exp(hd["tail"])
                                              + _dot_tn(k_tail.astype(BF16), hd["v_new"]))


def _scan_row_block(d, b, s, seq, batch, n_ctx):
    nc = n_ctx // CHUNK
    nl = seq // CHUNK
    ctx_pos = jnp.where(d == 0, s, nc - 1 - s)
    lat_pos = jnp.where(d == 0, s - nc, nl - 1 - (s - nc))
    return jnp.where(s < nc, (batch * seq + b * n_ctx) // CHUNK + ctx_pos, b * nl + lat_pos)


def _group_gates(x, n_heads, group, rows):
    ng = n_heads // group
    x = x.reshape(rows, 2, 2, ng, group).transpose(2, 1, 3, 0, 4)
    x = jnp.pad(x, ((0, 0), (0, 0), (0, 0), (0, 0), (0, LANES - group)))
    return x[0], x[1]


def _gdn_scan(act, beta, g, dims, n_ctx, rows):
    d_model, seq, batch = dims
    rep = GDN_V_HEADS // GDN_K_HEADS
    ng = GDN_K_HEADS // GDN_GROUP
    gw = GDN_GROUP * GDN_DK
    vw = GDN_GROUP * rep * GDN_DV
    k_blk0 = GDN_K_HEADS * GDN_DK // gw
    v_blk0 = 2 * GDN_K_HEADS * GDN_DK // vw
    steps = (n_ctx + seq) // CHUNK
    rb = functools.partial(_scan_row_block, seq=seq, batch=batch, n_ctx=n_ctx)
    in_specs = []
    for d in range(2):
        in_specs += [pl.BlockSpec((CHUNK, gw), lambda b, hg, s, d=d: (rb(d, b, s), hg)),
                     pl.BlockSpec((CHUNK, gw), lambda b, hg, s, d=d: (rb(d, b, s), k_blk0 + hg)),
                     pl.BlockSpec((CHUNK, vw), lambda b, hg, s, d=d: (rb(d, b, s), v_blk0 + hg)),
                     pl.BlockSpec((None, None, CHUNK, LANES), lambda b, hg, s, d=d: (d, hg, rb(d, b, s), 0)),
                     pl.BlockSpec((None, None, CHUNK, LANES), lambda b, hg, s, d=d: (d, hg, rb(d, b, s), 0))]
    out = jax.ShapeDtypeStruct((rows, GDN_V_HEADS * GDN_DV), F32)
    return pl.pallas_call(
        _gdn_scan_kernel,
        out_shape=(out, out),
        grid=(batch, ng, steps),
        in_specs=in_specs,
        out_specs=tuple(pl.BlockSpec((CHUNK, vw), lambda b, hg, s, d=d: (rb(d, b, s), hg)) for d in range(2)),
        scratch_shapes=[pltpu.VMEM((2 * GDN_GROUP * rep, GDN_DK, GDN_DV), F32)],
        compiler_params=_cparams(("parallel", "parallel", "arbitrary")),
        name="gdn_scan",
    )(act, act, act, beta, g, act, act, act, beta, g)


def _gdn_out_kernel(of_ref, ob_ref, z_ref, res_ref, gn_ref, gate_ref, w_ref, out_ref, acc_ref):
    k = pl.program_id(1)

    @pl.when(k == 0)
    def _():
        acc_ref[...] = jnp.zeros_like(acc_ref)

    o = of_ref[...] + ob_ref[...]
    z = z_ref[...]
    gn = gn_ref[...]
    parts = []
    for h in range(o.shape[1] // GDN_DV):
        seg = o[:, h * GDN_DV:(h + 1) * GDN_DV]
        zz = z[:, h * GDN_DV:(h + 1) * GDN_DV]
        seg = seg * lax.rsqrt(jnp.mean(seg * seg, -1, keepdims=True) + EPS) * gn
        parts.append((seg * (zz * _sigmoid(zz))).astype(BF16))
    acc_ref[...] += _dot(jnp.concatenate(parts, axis=1), w_ref[...])

    @pl.when(k == pl.num_programs(1) - 1)
    def _():
        out_ref[...] = res_ref[...] + gate_ref[0] * acc_ref[...]


def _gdn_out(o, qkvz, res, out_norm, mods, w_o, dims, rows, tm, tkk):
    d = dims[0]
    width = o[0].shape[1]
    z_blk0 = (qkvz.shape[1] - width) // tkk
    return pl.pallas_call(
        _gdn_out_kernel,
        out_shape=jax.ShapeDtypeStruct((rows, d), F32),
        grid=(rows // tm, width // tkk),
        in_specs=[pl.BlockSpec((tm, tkk), lambda i, k: (i, k)),
                  pl.BlockSpec((tm, tkk), lambda i, k: (i, k)),
                  pl.BlockSpec((tm, tkk), lambda i, k: (i, z_blk0 + k)),
                  pl.BlockSpec((tm, d), lambda i, k: (i, 0)),
                  pl.BlockSpec((1, GDN_DV), lambda i, k: (0, 0)),
                  _mod_spec(dims, tm, 2, 2),
                  pl.BlockSpec((tkk, d), lambda i, k: (k, 0))],
        out_specs=pl.BlockSpec((tm, d), lambda i, k: (i, 0)),
        scratch_shapes=[pltpu.VMEM((tm, d), F32)],
        compiler_params=_cparams(("parallel", "arbitrary")),
        name="gdn_out",
    )(o[0], o[1], qkvz, res, out_norm.reshape(1, GDN_DV), mods, w_o)


def _gdn_layer(hs, mods, w, dims, n_ctx):
    d, seq, batch = dims
    rows = hs.shape[0]
    norm1, w_in, conv_w, a_log, dt_bias, out_norm, w_o = w
    main_w = 2 * GDN_K_HEADS * GDN_DK + 2 * GDN_V_HEADS * GDN_DV
    qkvz = _nm_matmul(hs, norm1, mods, (0, 1), w_in[:, :main_w].astype(BF16), dims, rows, 512, 2048, name="gdn_w_in")
    gates = _nm_matmul(hs, norm1, mods, (0, 1), w_in[:, main_w:].astype(BF16), dims, rows, 512, LANES,
                       name="gdn_w_gates")
    act = _gdn_conv(qkvz, conv_w, dims, n_ctx, rows, 256, 1024)
    bg = _gdn_gates(gates, a_log, dt_bias, rows, 512)
    beta, g = _group_gates(bg, GDN_V_HEADS, GDN_GROUP * GDN_V_HEADS // GDN_K_HEADS, rows)
    o = _gdn_scan(act, beta, g, dims, n_ctx, rows)
    return _gdn_out(o, qkvz, hs, out_norm, mods, w_o.astype(BF16), dims, rows, 512, 1024)


def _ml_gate_kernel(x_ref, b_ref, o_ref):
    x = x_ref[...] + b_ref[...]
    lane = lax.broadcasted_iota(jnp.int32, x.shape, 1)
    o_ref[...] = jnp.where(lane % (2 * ML_HEADS) >= ML_HEADS, _log_sigmoid(x), x)


def _ml_gates(gates, gate_b, rows, tm):
    b_row = jnp.pad(gate_b.reshape(1, -1), ((0, 0), (0, LANES - 4 * ML_HEADS)))
    return pl.pallas_call(
        _ml_gate_kernel,
        out_shape=jax.ShapeDtypeStruct((rows, LANES), F32),
        grid=(rows // tm,),
        in_specs=[pl.BlockSpec((tm, LANES), lambda i: (i, 0)),
                  pl.BlockSpec((1, LANES), lambda i: (0, 0))],
        out_specs=pl.BlockSpec((tm, LANES), lambda i: (i, 0)),
        compiler_params=_cparams(("parallel",)),
        name="mlstm_gates",
    )(gates, b_row)


def _ml_scan_kernel(qf_ref, kf_ref, vf_ref, if_ref, ff_ref, qb_ref, kb_ref, vb_ref, ib_ref, fb_ref,
                    of_ref, ob_ref, c_ref, n_ref, m_ref):
    @pl.when(pl.program_id(2) == 0)
    def _():
        c_ref[...] = jnp.zeros_like(c_ref)
        n_ref[...] = jnp.zeros_like(n_ref)
        m_ref[...] = jnp.zeros_like(m_ref)

    streams = ((qf_ref, kf_ref, vf_ref, if_ref, ff_ref, of_ref), (qb_ref, kb_ref, vb_ref, ib_ref, fb_ref, ob_ref))
    row = lax.broadcasted_iota(jnp.int32, (CHUNK, CHUNK), 0)
    col = lax.broadcasted_iota(jnp.int32, (CHUNK, CHUNK), 1)

    heads = []
    for d, (q_ref, k_ref, v_ref, li_ref, lf_ref, o_ref) in enumerate(streams):
        incl = (row - col) * (1 - 2 * d) >= 0
        lf = lf_ref[...]
        li = li_ref[...]
        bc, bct = _cumsum_tables(lf, d)
        b_tot = jnp.sum(lf, axis=0, keepdims=True)
        lit = jnp.concatenate([li, li], axis=0).T
        for h in range(ML_GROUP):
            qh = q_ref[:, h * ML_DQK:(h + 1) * ML_DQK]
            kh = k_ref[:, h * ML_DQK:(h + 1) * ML_DQK] * (ML_DQK ** -0.5)
            bcol = bc[:, h:h + 1]
            b_last = b_tot[:, h:h + 1]
            brow = bct[h:h + 1, :CHUNK]
            irow = lit[h:h + 1, :CHUNK]
            d_mat = jnp.where(incl, bcol - brow + irow, -jnp.inf)
            w_end = b_last - bcol + li[:, h:h + 1]
            heads.append(dict(idx=d * ML_GROUP + h, h=h, q_ref=q_ref, v_ref=v_ref, o_ref=o_ref, kh=kh,
                              kht=jnp.concatenate([kh, kh], axis=0).T[:, :CHUNK], w_end_row=b_last - brow + irow,
                              bcol=bcol, b_last=b_last, d_mat=d_mat,
                              d_max=jnp.max(d_mat, axis=-1, keepdims=True),
                              qk=_dot_nt(qh.astype(BF16), kh.astype(BF16)),
                              w_end=w_end, w_end_max=jnp.max(w_end, axis=0, keepdims=True)))

    for hd in heads:
        hd["m"] = m_ref[hd["idx"]][0:1, 0:1]
        hd["c"] = c_ref[hd["idx"]]
        hd["nv"] = n_ref[hd["idx"]]
        hd["v"] = hd["v_ref"][:, hd["h"] * ML_DV:(hd["h"] + 1) * ML_DV].astype(BF16)
        inter = hd["bcol"] + hd["m"]
        hd["mt"] = jnp.maximum(hd["d_max"], inter)
        hd["p"] = jnp.exp(hd["d_mat"] - hd["mt"]) * hd["qk"]
        hd["a_in"] = jnp.exp(inter - hd["mt"])
    for hd in heads:
        qh = hd["q_ref"][:, hd["h"] * ML_DQK:(hd["h"] + 1) * ML_DQK]
        num = _dot(hd["p"].astype(BF16), hd["v"]) + hd["a_in"] * _dot(qh.astype(BF16), hd["c"].astype(BF16))
        den = (jnp.sum(hd["p"], axis=-1, keepdims=True)
               + hd["a_in"] * jnp.sum(qh * hd["nv"], axis=-1, keepdims=True))
        hd["o_ref"][:, hd["h"] * ML_DV:(hd["h"] + 1) * ML_DV] = num / jnp.maximum(jnp.abs(den), jnp.exp(-hd["mt"]))
    for hd in heads:
        m_new = jnp.maximum(hd["b_last"] + hd["m"], hd["w_end_max"])
        carry_decay = jnp.exp(hd["b_last"] + hd["m"] - m_new)
        kw = hd["kh"] * jnp.exp(hd["w_end"] - m_new)
        kwt = hd["kht"] * jnp.exp(hd["w_end_row"] - m_new)
        c_ref[hd["idx"]] = carry_decay * hd["c"] + _dot(kwt.astype(BF16), hd["v"])
        n_ref[hd["idx"]] = carry_decay * hd["nv"] + jnp.sum(kw, axis=0, keepdims=True)
        m_ref[hd["idx"]] = jnp.broadcast_to(m_new, m_ref.shape[1:])


def _ml_scan(qkvo, li, lf, dims, n_ctx, rows):
    d_model, seq, batch = dims
    ng = ML_HEADS // ML_GROUP
    qw = ML_GROUP * ML_DQK
    vw = ML_GROUP * ML_DV
    k_blk0 = ML_HEADS * ML_DQK // qw
    v_blk0 = 2 * ML_HEADS * ML_DQK // vw
    steps = (n_ctx + seq) // CHUNK
    rb = functools.partial(_scan_row_block, seq=seq, batch=batch, n_ctx=n_ctx)
    in_specs = []
    for d in range(2):
        in_specs += [pl.BlockSpec((CHUNK, qw), lambda b, hg, s, d=d: (rb(d, b, s), hg)),
                     pl.BlockSpec((CHUNK, qw), lambda b, hg, s, d=d: (rb(d, b, s), k_blk0 + hg)),
                     pl.BlockSpec((CHUNK, vw), lambda b, hg, s, d=d: (rb(d, b, s), v_blk0 + hg)),
                     pl.BlockSpec((None, None, CHUNK, LANES), lambda b, hg, s, d=d: (d, hg, rb(d, b, s), 0)),
                     pl.BlockSpec((None, None, CHUNK, LANES), lambda b, hg, s, d=d: (d, hg, rb(d, b, s), 0))]
    out = jax.ShapeDtypeStruct((rows, ML_HEADS * ML_DV), F32)
    return pl.pallas_call(
        _ml_scan_kernel,
        out_shape=(out, out),
        grid=(batch, ng, steps),
        in_specs=in_specs,
        out_specs=tuple(pl.BlockSpec((CHUNK, vw), lambda b, hg, s, d=d: (rb(d, b, s), hg)) for d in range(2)),
        scratch_shapes=[pltpu.VMEM((2 * ML_GROUP, ML_DQK, ML_DV), F32),
                        pltpu.VMEM((2 * ML_GROUP, 1, ML_DQK), F32),
                        pltpu.VMEM((2 * ML_GROUP, SUBLANES, LANES), F32)],
        compiler_params=_cparams(("parallel", "parallel", "arbitrary")),
        name="mlstm_scan",
    )(qkvo, qkvo, qkvo, li, lf, qkvo, qkvo, qkvo, li, lf)


def _ml_out_kernel(hf_ref, hb_ref, og_ref, res_ref, gn_ref, gate_ref, w_ref, out_ref):
    hs = hf_ref[...] + hb_ref[...]
    og = og_ref[...]
    gn = gn_ref[...]
    parts = []
    for h in range(ML_HEADS):
        sl = slice(h * ML_DV, (h + 1) * ML_DV)
        seg = hs[:, sl]
        seg = seg * lax.rsqrt(jnp.mean(seg * seg, -1, keepdims=True) + EPS) * gn[:, sl]
        parts.append((seg * _sigmoid(og[:, sl])).astype(BF16))
    out_ref[...] = res_ref[...] + gate_ref[0] * _dot(jnp.concatenate(parts, axis=1), w_ref[...])


def _ml_out(hh, qkvo, res, out_norm, mods, w_o, dims, rows, tm):
    d = dims[0]
    width = hh[0].shape[1]
    og_blk = (qkvo.shape[1] - width) // width
    return pl.pallas_call(
        _ml_out_kernel,
        out_shape=jax.ShapeDtypeStruct((rows, d), F32),
        grid=(rows // tm,),
        in_specs=[pl.BlockSpec((tm, width), lambda i: (i, 0)),
                  pl.BlockSpec((tm, width), lambda i: (i, 0)),
                  pl.BlockSpec((tm, width), lambda i: (i, og_blk)),
                  pl.BlockSpec((tm, d), lambda i: (i, 0)),
                  pl.BlockSpec((1, width), lambda i: (0, 0)),
                  _mod_spec(dims, tm, 2, 1),
                  pl.BlockSpec((width, d), lambda i: (0, 0))],
        out_specs=pl.BlockSpec((tm, d), lambda i: (i, 0)),
        compiler_params=_cparams(("parallel",)),
        name="mlstm_out",
    )(hh[0], hh[1], qkvo, res, out_norm.reshape(1, width), mods, w_o)


def _mlstm_layer(hs, mods, w, dims, n_ctx):
    d, seq, batch = dims
    rows = hs.shape[0]
    norm1, w_in, gate_b, out_norm, w_o = w
    main_w = 2 * ML_HEADS * ML_DQK + 2 * ML_HEADS * ML_DV
    qkvo = _nm_matmul(hs, norm1, mods, (0, 1), w_in[:, :main_w].astype(BF16), dims, rows, 512, 2048,
                      name="mlstm_w_in")
    w_g = jnp.pad(w_in[:, main_w:], ((0, 0), (0, LANES - 4 * ML_HEADS))).astype(BF16)
    gates = _nm_matmul(hs, norm1, mods, (0, 1), w_g, dims, rows, 512, LANES, name="mlstm_w_gates")
    lg = _ml_gates(gates, gate_b, rows, 512)
    li, lf = _group_gates(lg[:, :4 * ML_HEADS], ML_HEADS, ML_GROUP, rows)
    hh = _ml_scan(qkvo, li, lf, dims, n_ctx, rows)
    return _ml_out(hh, qkvo, hs, out_norm, mods, w_o.astype(BF16), dims, rows, 512)


def _final_norm_kernel(x_ref, g_ref, o_ref):
    x = x_ref[...]
    o_ref[...] = x * lax.rsqrt(jnp.mean(x * x, -1, keepdims=True) + EPS) * g_ref[...]


def _final_norm(x, g, rows, tm):
    d = x.shape[1]
    return pl.pallas_call(
        _final_norm_kernel,
        out_shape=jax.ShapeDtypeStruct((rows, d), F32),
        grid=(rows // tm,),
        in_specs=[pl.BlockSpec((tm, d), lambda i: (i, 0)),
                  pl.BlockSpec((1, d), lambda i: (0, 0))],
        out_specs=pl.BlockSpec((tm, d), lambda i: (i, 0)),
        compiler_params=_cparams(("parallel",)),
        name="final_norm",
    )(x, g.reshape(1, d))


def kernel(x, c, ctx, c_ctx, l0_ada_w, l0_ada_b, l0_norm1, l0_mla_w_in, l0_mla_q_norm, l0_mla_kv_norm, l0_mla_w_q_up, l0_mla_w_kv_up, l0_mla_w_o, l0_norm2, l0_mlp_w1, l0_mlp_w2, l1_ada_w, l1_ada_b, l1_norm1, l1_gdn_w_in, l1_gdn_conv_w, l1_gdn_a_log, l1_gdn_dt_bias, l1_gdn_out_norm, l1_gdn_w_o, l1_norm2, l1_mlp_w1, l1_mlp_w2, l2_ada_w, l2_ada_b, l2_norm1, l2_mlstm_w_in, l2_mlstm_gate_b, l2_mlstm_out_norm, l2_mlstm_w_o, l2_norm2, l2_mlp_w1, l2_mlp_w2, l3_ada_w, l3_ada_b, l3_norm1, l3_mla_w_in, l3_mla_q_norm, l3_mla_kv_norm, l3_mla_w_q_up, l3_mla_w_kv_up, l3_mla_w_o, l3_norm2, l3_mlp_w1, l3_mlp_w2, final_norm):
    batch, seq, d = x.shape
    n_ctx = ctx.shape[1]
    dims = (d, seq, batch)
    lat_rows = batch * seq
    rows = lat_rows + batch * n_ctx
    layers = (
        ("mla", (l0_ada_w, l0_ada_b, l0_norm2, l0_mlp_w1, l0_mlp_w2),
         (l0_norm1, l0_mla_w_in, l0_mla_q_norm, l0_mla_kv_norm, l0_mla_w_q_up, l0_mla_w_kv_up, l0_mla_w_o)),
        ("gdn", (l1_ada_w, l1_ada_b, l1_norm2, l1_mlp_w1, l1_mlp_w2),
         (l1_norm1, l1_gdn_w_in, l1_gdn_conv_w, l1_gdn_a_log, l1_gdn_dt_bias, l1_gdn_out_norm, l1_gdn_w_o)),
        ("mlstm", (l2_ada_w, l2_ada_b, l2_norm2, l2_mlp_w1, l2_mlp_w2),
         (l2_norm1, l2_mlstm_w_in, l2_mlstm_gate_b, l2_mlstm_out_norm, l2_mlstm_w_o)),
        ("mla", (l3_ada_w, l3_ada_b, l3_norm2, l3_mlp_w1, l3_mlp_w2),
         (l3_norm1, l3_mla_w_in, l3_mla_q_norm, l3_mla_kv_norm, l3_mla_w_q_up, l3_mla_w_kv_up, l3_mla_w_o)),
    )
    cond = jnp.concatenate([c, c_ctx[None, :], jnp.zeros((SUBLANES - batch - 1, d), F32)], 0)
    cos, sin = _rope_tables(seq)
    ctx_pad = ((0, batch * n_ctx), (0, 0))
    rope = (jnp.pad(jnp.tile(cos, (batch, 1)), ctx_pad, constant_values=1.0),
            jnp.pad(jnp.tile(sin, (batch, 1)), ctx_pad))
    hs = jnp.concatenate([x.reshape(lat_rows, d), ctx.reshape(batch * n_ctx, d)], 0)
    for li, (kind, (ada_w, ada_b, norm2, mlp_w1, mlp_w2), mixer_w) in enumerate(layers):
        ctx_out = li < len(layers) - 1
        mods = _ada(cond, ada_w, ada_b)[:batch + 1].reshape(batch + 1, 1, N_MOD * d)
        if kind == "mla":
            hs = _mla_layer(hs, mods, mixer_w, dims, n_ctx, ctx_out, rope)
        elif kind == "gdn":
            hs = _gdn_layer(hs, mods, mixer_w, dims, n_ctx)
        else:
            hs = _mlstm_layer(hs, mods, mixer_w, dims, n_ctx)
        hs = _mlp(hs, norm2, mods, mlp_w1.astype(BF16), mlp_w2.astype(BF16), dims, hs.shape[0], 512, 1024)
    return _final_norm(hs, final_norm, lat_rows, 512).reshape(batch, seq, d)
```
